```python
import jax, jax.numpy as jnp
from jax import lax
import numpy as np

D_MODEL = 1024
BATCH = 16
SEQ = 4096
DEPTH = 1
DEC_BATCH = 128
DEC_SEQ = 8
PAST_LEN = 8192
PAGE_SIZE = 128

D_CONV = D_MODEL // 2
CONV_WIDTH = 3
N_HEADS = 8
HEAD_DIM = 64
N_KV_HEADS = 2
GROUP = N_HEADS // N_KV_HEADS
D_ATT = N_HEADS * HEAD_DIM
D_KV = N_KV_HEADS * HEAD_DIM
N_IDX_HEADS = 8
IDX_DIM = 64
TOPK_MAX = 256
Q_BLOCK = 128
EPS = 1e-6
NEG = -1e30
IN_SIZES = (D_CONV, D_CONV, D_CONV, D_CONV, D_ATT, D_KV, D_KV, D_ATT,
            N_IDX_HEADS * IDX_DIM, IDX_DIM, N_IDX_HEADS, D_MODEL, D_MODEL)
D_IN = sum(IN_SIZES)

kernel_name = "hybrid_conv_dsa_gated_decode_step"


def rmsnorm(x, g):
    xf = x.astype(jnp.float32)
    y = xf * lax.rsqrt(jnp.mean(xf * xf, axis=-1, keepdims=True) + EPS)
    return (y * g.astype(jnp.float32)).astype(x.dtype)


def split_cols(h):
    idx, acc = [], 0
    for s in IN_SIZES[:-1]:
        acc += s
        idx.append(acc)
    return jnp.split(h, idx, axis=-1)


def alibi_slopes():
    h = jnp.arange(N_HEADS, dtype=jnp.float32) + 1.0
    return jnp.exp2(-8.0 * h / N_HEADS).reshape(N_KV_HEADS, GROUP)


def short_conv(u, prev, conv_w):
    T = u.shape[1]
    up = jnp.concatenate([prev.astype(u.dtype), u], axis=1)
    y = conv_w[0] * up[:, 0:T]
    for j in range(1, CONV_WIDTH):
        y = y + conv_w[j] * up[:, j:j + T]
    return y, up[:, T:]


def indexer_select(qi, wi, kidx, q_pos, k_sel):
    L = kidx.shape[1]
    s = jnp.einsum('bqhd,bsd->bqhs', qi, kidx)
    s = jnp.einsum('bqhs,bqh->bqs', jax.nn.relu(s), wi).astype(jnp.float32)
    k_pos = jnp.arange(L)
    s = jnp.where(k_pos[None, None, :] <= q_pos[None, :, None], s, -jnp.inf)
    _, sel = lax.top_k(s, k_sel)
    return sel


def sparse_attend(q, kg, vg, sel, q_pos):
    B, Tq = q.shape[0], q.shape[1]
    s = jnp.einsum('bqhgd,bqkhd->bqhgk', q, kg).astype(jnp.float32) * (HEAD_DIM ** -0.5)
    dist = (q_pos[None, :, None] - sel).astype(jnp.float32)
    s = s - alibi_slopes()[None, None, :, :, None] * dist[:, :, None, None, :]
    valid = (sel <= q_pos[None, :, None])[:, :, None, None, :]
    p = jax.nn.softmax(jnp.where(valid, s, NEG), axis=-1).astype(vg.dtype)
    o = jnp.einsum('bqhgk,bqkhd->bqhgd', p, vg)
    return o.reshape(B, Tq, D_ATT)


def gather_rows(a, ix):
    return jax.vmap(lambda aa, ii: aa[ii])(a, ix)


def prompt_attention(q, k, v, qi, wi, kidx):
    B, S = q.shape[0], q.shape[1]
    k_sel = min(TOPK_MAX, S // 4)
    qb = min(Q_BLOCK, S)
    nb = S // qb

    def block(i):
        start = i * qb
        q_b = lax.dynamic_slice_in_dim(q, start, qb, axis=1)
        qi_b = lax.dynamic_slice_in_dim(qi, start, qb, axis=1)
        wi_b = lax.dynamic_slice_in_dim(wi, start, qb, axis=1)
        q_pos = start + jnp.arange(qb)
        sel = indexer_select(qi_b, wi_b, kidx, q_pos, k_sel)
        return sparse_attend(q_b, gather_rows(k, sel), gather_rows(v, sel), sel, q_pos)

    out = lax.map(block, jnp.arange(nb))
    return out.transpose(1, 0, 2, 3).reshape(B, S, D_ATT)


def sample_attention(q, k_new, v_new, qi, wi, kidx_new, cache_k, cache_v, cache_kidx, page_table):
    DB, T = q.shape[0], q.shape[1]
    past = page_table.shape[1] * PAGE_SIZE
    L = past + T
    kidx_past = cache_kidx[page_table].reshape(DB, past, IDX_DIM)
    kidx_all = jnp.concatenate([kidx_past.astype(kidx_new.dtype), kidx_new], axis=1)
    q_pos = past + jnp.arange(T)
    sel = indexer_select(qi, wi, kidx_all, q_pos, min(TOPK_MAX, L // 4))
    in_past = (sel < past)[..., None, None]
    ps = jnp.minimum(sel, past - 1)
    phys = jax.vmap(lambda pt, ii: pt[ii])(page_table, ps // PAGE_SIZE)
    off = ps % PAGE_SIZE
    ns = jnp.clip(sel - past, 0, T - 1)
    kg = jnp.where(in_past, cache_k[phys, off].astype(k_new.dtype), gather_rows(k_new, ns))
    vg = jnp.where(in_past, cache_v[phys, off].astype(v_new.dtype), gather_rows(v_new, ns))
    return sparse_attend(q, kg, vg, sel, q_pos)


def layer(x, norm_gain, w_in, conv_w, w_up_conv, w_up_attn, w_out, conv_prev, attend):
    B, T = x.shape[0], x.shape[1]
    xn = rmsnorm(x, norm_gain)
    h = jnp.einsum('btd,dn->btn', xn, w_in)
    cb, cc, cu, cz, q, k, v, az, iq, ik, iw, gc, ga = split_cols(h)
    conv_y, conv_state = short_conv(cc * cu, conv_prev, conv_w)
    y_conv = cb * conv_y * jax.nn.silu(cz)
    q = q.reshape(B, T, N_KV_HEADS, GROUP, HEAD_DIM)
    k = k.reshape(B, T, N_KV_HEADS, HEAD_DIM)
    v = v.reshape(B, T, N_KV_HEADS, HEAD_DIM)
    iq = iq.reshape(B, T, N_IDX_HEADS, IDX_DIM) * (IDX_DIM ** -0.5)
    iw = iw * (N_IDX_HEADS ** -0.5)
    y_attn = attend(q, k, v, iq, iw, ik) * jax.nn.silu(az)
    m = (jax.nn.sigmoid(gc) * jnp.einsum('btc,cd->btd', y_conv, w_up_conv)
         + jax.nn.sigmoid(ga) * jnp.einsum('bta,ad->btd', y_attn, w_up_attn))
    x = x + jnp.einsum('btd,de->bte', m, w_out)
    return x, k, v, ik, conv_state


def setup_inputs(seed: int = 0) -> dict:
    key = jax.random.key(seed)
    ks = jax.random.split(key, 16)
    n_pages = PAST_LEN // PAGE_SIZE
    used = DEC_BATCH * n_pages
    n_phys = used + max(1, used // 4)
    f = jnp.float32
    x_prompt = jax.random.normal(ks[0], (BATCH, SEQ, D_MODEL), f)
    x_sample = jax.random.normal(ks[1], (DEC_BATCH, DEC_SEQ, D_MODEL), f)
    cache_k = jax.random.normal(ks[2], (DEPTH, n_phys, PAGE_SIZE, N_KV_HEADS, HEAD_DIM), f)
    cache_v = jax.random.normal(ks[3], (DEPTH, n_phys, PAGE_SIZE, N_KV_HEADS, HEAD_DIM), f)
    cache_kidx = jax.random.normal(ks[4], (DEPTH, n_phys, PAGE_SIZE, IDX_DIM), f)
    state_conv = jax.random.normal(ks[5], (DEPTH, DEC_BATCH, CONV_WIDTH - 1, D_CONV), f)
    page_table = jax.random.permutation(ks[6], n_phys)[:used].reshape(DEC_BATCH, n_pages).astype(jnp.int32)
    norm_gain = 1.0 + 0.1 * jax.random.normal(ks[7], (DEPTH, D_MODEL), f)
    w_in = jax.random.normal(ks[8], (DEPTH, D_MODEL, D_IN), f) * D_MODEL ** -0.5
    conv_w = jax.random.normal(ks[9], (DEPTH, CONV_WIDTH, D_CONV), f) * CONV_WIDTH ** -0.5
    w_up_conv = jax.random.normal(ks[10], (DEPTH, D_CONV, D_MODEL), f) * D_CONV ** -0.5
    w_up_attn = jax.random.normal(ks[11], (DEPTH, D_ATT, D_MODEL), f) * D_ATT ** -0.5
    w_out = jax.random.normal(ks[12], (DEPTH, D_MODEL, D_MODEL), f) * D_MODEL ** -0.5
    final_gain = 1.0 + 0.1 * jax.random.normal(ks[13], (D_MODEL,), f)
    return {"x_prompt": x_prompt, "x_sample": x_sample, "cache_k": cache_k, "cache_v": cache_v,
            "cache_kidx": cache_kidx, "state_conv": state_conv, "page_table": page_table,
            "norm_gain": norm_gain, "w_in": w_in, "conv_w": conv_w, "w_up_conv": w_up_conv,
            "w_up_attn": w_up_attn, "w_out": w_out, "final_gain": final_gain}


def reference(x_prompt, x_sample, cache_k, cache_v, cache_kidx, state_conv, page_table,
              norm_gain, w_in, conv_w, w_up_conv, w_up_attn, w_out, final_gain):
    xp, xs = x_prompt, x_sample
    kp_l, vp_l, ip_l, cp_l, ks_l, vs_l, is_l, cs_l = [], [], [], [], [], [], [], []
    for l in range(DEPTH):
        prev_p = jnp.zeros((xp.shape[0], CONV_WIDTH - 1, D_CONV), xp.dtype)
        xp, kp, vp, ip, cp = layer(xp, norm_gain[l], w_in[l], conv_w[l], w_up_conv[l], w_up_attn[l],
                                   w_out[l], prev_p, prompt_attention)
        ck, cv, ci = cache_k[l], cache_v[l], cache_kidx[l]
        attend_s = lambda q, k, v, qi, wi, ki, ck=ck, cv=cv, ci=ci: sample_attention(
            q, k, v, qi, wi, ki, ck, cv, ci, page_table)
        xs, ks_, vs_, is_, cs = layer(xs, norm_gain[l], w_in[l], conv_w[l], w_up_conv[l], w_up_attn[l],
                                      w_out[l], state_conv[l], attend_s)
        kp_l.append(kp); vp_l.append(vp); ip_l.append(ip); cp_l.append(cp)
        ks_l.append(ks_); vs_l.append(vs_); is_l.append(is_); cs_l.append(cs)
    y_prompt = rmsnorm(xp, final_gain)
    y_sample = rmsnorm(xs, final_gain)
    return (y_prompt, y_sample,
            jnp.stack(kp_l), jnp.stack(vp_l), jnp.stack(ip_l), jnp.stack(cp_l),
            jnp.stack(ks_l), jnp.stack(vs_l), jnp.stack(is_l), jnp.stack(cs_l))
```

```python
import functools

import jax
import jax.numpy as jnp
from jax import lax
from jax.experimental import pallas as pl
from jax.experimental.pallas import tpu as pltpu

D_MODEL = 1024
D_CONV = 512
CONV_WIDTH = 3
N_HEADS = 8
HEAD_DIM = 64
N_KV_HEADS = 2
GROUP = N_HEADS // N_KV_HEADS
D_ATT = N_HEADS * HEAD_DIM
D_KV = N_KV_HEADS * HEAD_DIM
N_IDX_HEADS = 8
IDX_DIM = 64
TOPK_MAX = 256
PAGE_SIZE = 128
EPS = 1e-6

LANES = 128
SUBLANES = 8
VMEM_LIMIT = 48 * 1024 * 1024
MXU_DTYPE = jnp.bfloat16

C_CONV = 0
C_Q = 4 * D_CONV
C_K = C_Q + D_ATT
C_V = C_K + D_KV
C_AZ = C_V + D_KV
C_IQ = C_AZ + D_ATT
C_IKW = C_IQ + N_IDX_HEADS * IDX_DIM
C_G = C_IKW + LANES
D_IN_RAW = C_IKW + IDX_DIM + N_IDX_HEADS + 2 * D_MODEL
D_IN_PAD = C_G + 2 * D_MODEL
D_KEYS_T = 2 * D_KV + IDX_DIM

MASKED_DIST = -1e33
INT_MIN = -2 ** 31
KEY_NEG_INF = INT_MIN + 0x7FFFFF


def _dot(a, b):
    return jnp.dot(a, b, preferred_element_type=jnp.float32)


def _dot_nt(a, b):
    return lax.dot_general(a, b, (((1,), (1,)), ((), ())), preferred_element_type=jnp.float32)


def _mxu(x):
    return x.astype(MXU_DTYPE)


def _proj_kernel(x_ref, prev_ref, g_ref, w_ref, wt_ref, cw_ref, *refs, tm, keys_on_lanes):
    yc_ref, q_ref, az_ref, iq_ref, ikw_ref, gg_ref, cs_ref, k_ref, v_ref = refs[:9]
    u_buf = refs[-1]
    t = pl.program_id(1)
    nt = pl.num_programs(1)
    x = x_ref[0]
    xn = x * lax.rsqrt(jnp.mean(x * x, axis=-1, keepdims=True) + EPS) * g_ref[...]
    xn = _mxu(xn)

    def proj(c0, width):
        return _dot(xn, w_ref[:, c0:c0 + width])

    @pl.when(t == 0)
    def _():
        u_buf[0:SUBLANES, :] = jnp.zeros((SUBLANES, D_CONV), jnp.float32)
        u_buf[SUBLANES - 2:SUBLANES, :] = prev_ref[0]

    @pl.when(t > 0)
    def _():
        u_buf[0:SUBLANES, :] = u_buf[tm:tm + SUBLANES, :]

    u_buf[SUBLANES:SUBLANES + tm, :] = proj(C_CONV + D_CONV, D_CONV) * proj(C_CONV + 2 * D_CONV, D_CONV)
    conv = (cw_ref[0:1, :] * u_buf[SUBLANES - 2:SUBLANES - 2 + tm, :]
            + cw_ref[1:2, :] * u_buf[SUBLANES - 1:SUBLANES - 1 + tm, :]
            + cw_ref[2:3, :] * u_buf[SUBLANES:SUBLANES + tm, :])
    yc_ref[0] = (proj(C_CONV, D_CONV) * conv * jax.nn.silu(proj(C_CONV + 3 * D_CONV, D_CONV))).astype(yc_ref.dtype)

    @pl.when(t == nt - 1)
    def _():
        cs_ref[0] = u_buf[tm + SUBLANES - 2:tm + SUBLANES, :]

    q_ref[0] = (proj(C_Q, D_ATT) * (HEAD_DIM ** -0.5)).astype(q_ref.dtype)
    az_ref[0] = proj(C_AZ, D_ATT)
    iq_ref[0] = (proj(C_IQ, N_IDX_HEADS * IDX_DIM) * (IDX_DIM ** -0.5)).astype(iq_ref.dtype)
    ikw_ref[0] = proj(C_IKW, LANES)
    gg_ref[0] = proj(C_G, 2 * D_MODEL)
    if keys_on_lanes:
        ki_ref = refs[9]
        k_ref[0] = _dot_nt(wt_ref[0:D_KV, :], xn)
        v_ref[0] = _dot_nt(wt_ref[D_KV:2 * D_KV, :], xn)
        ki_ref[0] = _dot_nt(wt_ref[2 * D_KV:D_KEYS_T, :], xn)
    else:
        k_ref[0] = proj(C_K, D_KV)
        v_ref[0] = proj(C_V, D_KV)


def _project(x, prev, norm_gain, w_pad, wt_keys, conv_w, *, tm, qdtype, keys_on_lanes):
    b, t, _ = x.shape
    nt = t // tm
    row = lambda width: pl.BlockSpec((1, tm, width), lambda i, j: (i, j, 0))
    col = lambda height: pl.BlockSpec((1, height, tm), lambda i, j: (i, 0, j))
    full = lambda shape: pl.BlockSpec(shape, lambda i, j: (0,) * len(shape))
    state = pl.BlockSpec((1, CONV_WIDTH - 1, D_CONV), lambda i, j: (i, 0, 0))
    f32 = jnp.float32
    out_shapes = [
        jax.ShapeDtypeStruct((b, t, D_CONV), MXU_DTYPE),
        jax.ShapeDtypeStruct((b, t, D_ATT), qdtype),
        jax.ShapeDtypeStruct((b, t, D_ATT), f32),
        jax.ShapeDtypeStruct((b, t, N_IDX_HEADS * IDX_DIM), qdtype),
        jax.ShapeDtypeStruct((b, t, LANES), f32),
        jax.ShapeDtypeStruct((b, t, 2 * D_MODEL), f32),
        jax.ShapeDtypeStruct((b, CONV_WIDTH - 1, D_CONV), f32),
    ]
    out_specs = [row(D_CONV), row(D_ATT), row(D_ATT), row(N_IDX_HEADS * IDX_DIM), row(LANES),
                 row(2 * D_MODEL), state]
    if keys_on_lanes:
        out_shapes += [jax.ShapeDtypeStruct((b, D_KV, t), f32), jax.ShapeDtypeStruct((b, D_KV, t), f32),
                       jax.ShapeDtypeStruct((b, IDX_DIM, t), f32)]
        out_specs += [col(D_KV), col(D_KV), col(IDX_DIM)]
    else:
        out_shapes += [jax.ShapeDtypeStruct((b, t, D_KV), f32), jax.ShapeDtypeStruct((b, t, D_KV), f32)]
        out_specs += [row(D_KV), row(D_KV)]
    return pl.pallas_call(
        functools.partial(_proj_kernel, tm=tm, keys_on_lanes=keys_on_lanes),
        grid=(b, nt),
        in_specs=[row(D_MODEL), state, full((1, D_MODEL)), full((D_MODEL, D_IN_PAD)),
                  full((D_KEYS_T, D_MODEL)), full((CONV_WIDTH, D_CONV))],
        out_specs=out_specs,
        out_shape=out_shapes,
        scratch_shapes=[pltpu.VMEM((tm + SUBLANES, D_CONV), f32)],
        compiler_params=pltpu.CompilerParams(
            dimension_semantics=("arbitrary", "arbitrary"), vmem_limit_bytes=VMEM_LIMIT),
        name="proj",
    )(x, prev, norm_gain.reshape(1, D_MODEL), w_pad, wt_keys, conv_w)


def _merge_kernel(x_ref, yc_ref, at_ref, az_ref, gg_ref, wuc_ref, wua_ref, wo_ref, fg_ref, y_ref):
    ya = _mxu(at_ref[...] * jax.nn.silu(az_ref[...]))
    gc = gg_ref[:, 0:D_MODEL]
    ga = gg_ref[:, D_MODEL:2 * D_MODEL]
    m = (jax.nn.sigmoid(gc) * _dot(yc_ref[...], wuc_ref[...])
         + jax.nn.sigmoid(ga) * _dot(ya, wua_ref[...]))
    xo = x_ref[...] + _dot(_mxu(m), wo_ref[...])
    y_ref[...] = xo * lax.rsqrt(jnp.mean(xo * xo, axis=-1, keepdims=True) + EPS) * fg_ref[...]


def _merge(x, yc, attn, az, gg, wuc, wua, wo, final_gain, *, tm):
    n = x.shape[0]
    row = lambda width: pl.BlockSpec((tm, width), lambda i: (i, 0))
    full = lambda shape: pl.BlockSpec(shape, lambda i: (0,) * len(shape))
    return pl.pallas_call(
        _merge_kernel,
        grid=(n // tm,),
        in_specs=[row(D_MODEL), row(D_CONV), row(D_ATT), row(D_ATT), row(2 * D_MODEL),
                  full((D_CONV, D_MODEL)), full((D_ATT, D_MODEL)), full((D_MODEL, D_MODEL)),
                  full((1, D_MODEL))],
        out_specs=row(D_MODEL),
        out_shape=jax.ShapeDtypeStruct((n, D_MODEL), jnp.float32),
        compiler_params=pltpu.CompilerParams(
            dimension_semantics=("arbitrary",), vmem_limit_bytes=VMEM_LIMIT),
        name="merge",
    )(x, yc, attn, az, gg, wuc, wua, wo, final_gain.reshape(1, D_MODEL))


def _key_to_threshold(tu):
    key = tu ^ jnp.int32(INT_MIN)
    bits = key ^ ((key >> 31) & jnp.int32(0x7FFFFFFF))
    thr = lax.bitcast_convert_type(bits, jnp.float32)
    return jnp.where(key < jnp.int32(KEY_NEG_INF), -jnp.inf, thr)


def _lane_column(x, lane):
    ids = lax.broadcasted_iota(jnp.int32, x.shape, 1)
    return jnp.sum(jnp.where(ids == lane, x, 0.0), axis=1, keepdims=True)


def _swap_halves(x):
    return pltpu.roll(x, LANES // 2, axis=1)


def _head_slope(h):
    return 2.0 ** (-8.0 * (h + 1) / N_HEADS)


def _pack_head_pair(a0, a1, kv_head):
    lo = lax.broadcasted_iota(jnp.int32, a0.shape, 1) < LANES // 2
    if kv_head == 0:
        return jnp.where(lo, a0, _swap_halves(a1))
    return jnp.where(lo, _swap_halves(a0), a1)


def _attn_prompt_kernel(q_ref, iq_ref, ikwq_ref, kt_ref, vt_ref, kit_ref, o_ref,
                        kv_s, v_s, ki_s, sc_s, dist_s, m_s, l_s, acc_s, j_s,
                        *, seq, tq, ck, k_sel):
    i = pl.program_id(1)
    n_ck = ((i + 1) * tq + ck - 1) // ck
    half = LANES // 2

    @pl.when(i == 0)
    def _():
        zeros = jnp.zeros((half, ck), MXU_DTYPE)
        for c in range(seq // ck):
            cols = slice(c * ck, (c + 1) * ck)
            kt = _mxu(kt_ref[0, :, cols])
            for g in range(N_KV_HEADS):
                for par in range(2):
                    kv_s[2 * g + par, c, par * half:(par + 1) * half, :] = kt[g * half:(g + 1) * half]
                    kv_s[2 * g + par, c, (1 - par) * half:(2 - par) * half, :] = zeros
            v_s[c] = _mxu(vt_ref[0, :, cols])
            ki = _mxu(kit_ref[0, :, cols])
            for par in range(2):
                ki_s[par, c, par * half:(par + 1) * half, :] = ki
                ki_s[par, c, (1 - par) * half:(2 - par) * half, :] = zeros

    q_pos = i * tq + lax.broadcasted_iota(jnp.int32, (tq, ck), 0)
    k_off = lax.broadcasted_iota(jnp.int32, (tq, ck), 1)

    ikwq = ikwq_ref[0]
    w_cols = [_lane_column(ikwq, IDX_DIM + h) * (N_IDX_HEADS ** -0.5) for h in range(N_IDX_HEADS)]

    def score_chunk(c, carry):
        acc = jnp.zeros((tq, ck), jnp.float32)
        for h in range(N_IDX_HEADS):
            lhs = iq_ref[0, :, (h // 2) * LANES:(h // 2 + 1) * LANES]
            acc = acc + jnp.maximum(_dot(lhs, ki_s[h % 2, c]), 0.0) * w_cols[h]
        sc_s[c] = jnp.where(c * ck + k_off <= q_pos, acc, -jnp.inf)
        return carry
    lax.fori_loop(0, n_ck, score_chunk, 0)

    tile_lane = lax.broadcasted_iota(jnp.int32, (tq, LANES), 1)

    def count(pred):
        def body(c, part):
            for t in range(ck // LANES):
                s_tile = sc_s[c, :, t * LANES:(t + 1) * LANES]
                part = part + jnp.where(pred(s_tile, c * ck + t * LANES + tile_lane), 1.0, 0.0)
            return part
        part = lax.fori_loop(0, n_ck, body, jnp.zeros((tq, LANES), jnp.float32))
        return jnp.sum(part, axis=1, keepdims=True)

    def lanes(column):
        return jnp.broadcast_to(column, (tq, LANES))

    def bisect(it, carry):
        tu, cnt_acc = carry
        cand = tu | jnp.left_shift(jnp.int32(1), 31 - it)
        thr_c = lanes(_key_to_threshold(cand))
        cnt = count(lambda s, kp: s >= thr_c)
        ok = cnt >= float(k_sel)
        return jnp.where(ok, cand, tu), jnp.where(ok, cnt, cnt_acc)

    n_all = (n_ck * ck).astype(jnp.float32)
    tu, cnt_ge = lax.fori_loop(
        0, 32, bisect, (jnp.zeros((tq, 1), jnp.int32), jnp.zeros((tq, 1), jnp.float32) + n_all))
    thr = _key_to_threshold(tu)

    j_s[...] = jnp.full((tq, 1), seq, jnp.int32)

    @pl.when(jnp.max(cnt_ge) > float(k_sel))
    def _():
        thr_l = lanes(thr)
        need = float(k_sel) - count(lambda s, kp: s > thr_l)
        nbits = (seq - 1).bit_length()
        def index_bit(it, p):
            cand = p + jnp.left_shift(jnp.int32(1), nbits - 1 - it)
            cand_l = lanes(cand)
            below = count(lambda s, kp: (s == thr_l) & (kp < cand_l))
            return jnp.where(below < need, cand, p)
        j_s[...] = lax.fori_loop(0, nbits, index_bit, jnp.zeros((tq, 1), jnp.int32))

    m_s[...] = jnp.full(m_s.shape, -jnp.inf, jnp.float32)
    l_s[...] = jnp.zeros(l_s.shape, jnp.float32)
    acc_s[...] = jnp.zeros(acc_s.shape, jnp.float32)
    j_last = j_s[...]

    def attend_chunk(c, carry):
        k_pos = c * ck + k_off
        s_idx = sc_s[c]
        sel = ((s_idx > thr) | ((s_idx == thr) & (k_pos <= j_last))) & (k_pos <= q_pos)
        dist_s[...] = jnp.where(sel, (k_pos - q_pos).astype(jnp.float32), MASKED_DIST)
        vt = v_s[c]
        for h in range(N_HEADS):
            g, par = h // GROUP, h % 2
            lhs = q_ref[0, :, (h // 2) * LANES:(h // 2 + 1) * LANES]
            s = _dot(lhs, kv_s[2 * g + par, c]) + _head_slope(h) * dist_s[...]
            m_old = m_s[h]
            m_new = jnp.maximum(m_old, jnp.max(s, axis=1, keepdims=True))
            alpha = jnp.exp(m_old - m_new)
            p = jnp.exp(s - m_new)
            l_s[h] = alpha * l_s[h] + jnp.sum(p, axis=1, keepdims=True)
            acc_s[h] = alpha * acc_s[h] + _dot_nt(_mxu(p), vt)
            m_s[h] = m_new
        return carry
    lax.fori_loop(0, n_ck, attend_chunk, 0)

    for j in range(N_HEADS // 2):
        o_ref[0, :, j * LANES:(j + 1) * LANES] = _pack_head_pair(
            acc_s[2 * j] / l_s[2 * j], acc_s[2 * j + 1] / l_s[2 * j + 1], (2 * j) // GROUP)


def _attn_prompt(q, iq, ikw, kt, vt, kit, *, tq, ck):
    b, seq, _ = q.shape
    k_sel = min(TOPK_MAX, seq // 4)
    n_ck = seq // ck
    blk = lambda width: pl.BlockSpec((1, tq, width), lambda i, j: (i, j, 0))
    whole = lambda height: pl.BlockSpec((1, height, seq), lambda i, j: (i, 0, 0))
    f32 = jnp.float32
    return pl.pallas_call(
        functools.partial(_attn_prompt_kernel, seq=seq, tq=tq, ck=ck, k_sel=k_sel),
        grid=(b, seq // tq),
        in_specs=[blk(D_ATT), blk(N_IDX_HEADS * IDX_DIM), blk(LANES),
                  whole(D_KV), whole(D_KV), whole(IDX_DIM)],
        out_specs=blk(D_ATT),
        out_shape=jax.ShapeDtypeStruct((b, seq, D_ATT), f32),
        scratch_shapes=[
            pltpu.VMEM((2 * N_KV_HEADS, n_ck, LANES, ck), MXU_DTYPE),
            pltpu.VMEM((n_ck, D_KV, ck), MXU_DTYPE),
            pltpu.VMEM((2, n_ck, LANES, ck), MXU_DTYPE),
            pltpu.VMEM((n_ck, tq, ck), f32),
            pltpu.VMEM((tq, ck), f32),
            pltpu.VMEM((N_HEADS, tq, 1), f32),
            pltpu.VMEM((N_HEADS, tq, 1), f32),
            pltpu.VMEM((N_HEADS, tq, LANES), f32),
            pltpu.VMEM((tq, 1), jnp.int32),
        ],
        compiler_params=pltpu.CompilerParams(
            dimension_semantics=("arbitrary", "arbitrary"), vmem_limit_bytes=VMEM_LIMIT),
        name="attn_prompt",
    )(q, iq, ikw, kt, vt, kit)


def _attn_sample_kernel(pt_ref, q_ref, iq_ref, ikw_ref, k_ref, v_ref, ck_hbm, cv_hbm, ci_hbm, o_ref,
                        kbuf, vbuf, ibuf, sc_s, s_s, sem,
                        *, n_pages, t_new, kc, k_sel):
    b = pl.program_id(0)
    past = n_pages * PAGE_SIZE
    n_rows = N_HEADS * t_new
    f32 = jnp.float32

    def page_copies(p):
        phys = pt_ref[b, p]
        cols = pl.ds(pl.multiple_of(p * PAGE_SIZE, PAGE_SIZE), PAGE_SIZE)
        return (pltpu.make_async_copy(ck_hbm.at[phys], kbuf.at[:, cols], sem.at[0]),
                pltpu.make_async_copy(cv_hbm.at[phys], vbuf.at[:, cols], sem.at[1]),
                pltpu.make_async_copy(ci_hbm.at[phys], ibuf.at[:, cols], sem.at[2]))

    def start_page(p, carry):
        for cp in page_copies(p):
            cp.start()
        return carry
    lax.fori_loop(0, n_pages, start_page, 0)

    lo = lax.broadcasted_iota(jnp.int32, (t_new, LANES), 1) < LANES // 2
    ikw = ikw_ref[0]
    iq_rows, q_rows, w_rows = [], [], []
    for h in range(N_HEADS):
        it = iq_ref[0, :, (h // 2) * LANES:(h // 2 + 1) * LANES]
        iq_rows.append((it if h % 2 == 0 else _swap_halves(it))[:, 0:IDX_DIM])
        qt = q_ref[0, :, (h // 2) * LANES:(h // 2 + 1) * LANES]
        g, par = h // GROUP, h % 2
        if g != par:
            qt = _swap_halves(qt)
        q_rows.append(jnp.where(lo, qt, 0.0) if g == 0 else jnp.where(lo, 0.0, qt))
        w_rows.append(_lane_column(ikw, IDX_DIM + h) * (N_IDX_HEADS ** -0.5))
    iq_all = _mxu(jnp.concatenate(iq_rows, axis=0))
    q_all = _mxu(jnp.concatenate(q_rows, axis=0))
    w_all = jnp.concatenate(w_rows, axis=0)
    slope = jnp.concatenate(
        [jnp.full((t_new, 1), _head_slope(h), f32) for h in range(N_HEADS)], axis=0)

    pad = jnp.zeros((LANES - t_new, LANES), f32)
    ik_new = _mxu(jnp.concatenate([ikw, pad], axis=0)[:, 0:IDX_DIM])
    k_new = _mxu(jnp.concatenate([k_ref[0], pad], axis=0))
    v_new = _mxu(jnp.concatenate([v_ref[0], pad], axis=0))

    def head_sum(r):
        acc = r[0:t_new]
        for h in range(1, N_HEADS):
            acc = acc + r[h * t_new:(h + 1) * t_new]
        return acc

    def wait_page(p, carry):
        for cp in page_copies(p):
            cp.wait()
        return carry
    lax.fori_loop(0, n_pages, wait_page, 0)

    for c in range(past // kc):
        cols = slice(c * kc, (c + 1) * kc)
        sc_s[:, cols] = head_sum(jnp.maximum(_dot(iq_all, _mxu(ibuf[:, cols])), 0.0) * w_all)
    t_id = lax.broadcasted_iota(jnp.int32, (t_new, LANES), 0)
    j_id = lax.broadcasted_iota(jnp.int32, (t_new, LANES), 1)
    s_new = head_sum(jnp.maximum(_dot_nt(iq_all, ik_new), 0.0) * w_all)
    sc_s[:, past:past + LANES] = jnp.where(j_id <= t_id, s_new, -jnp.inf)

    sc = sc_s[...]
    def count(mask):
        return jnp.sum(jnp.where(mask, 1.0, 0.0), axis=1, keepdims=True)

    def bisect(it, carry):
        tu, cnt_acc = carry
        cand = tu | jnp.left_shift(jnp.int32(1), 31 - it)
        cnt = count(sc >= _key_to_threshold(cand))
        ok = cnt >= float(k_sel)
        return jnp.where(ok, cand, tu), jnp.where(ok, cnt, cnt_acc)

    tu, cnt_ge = lax.fori_loop(
        0, 32, bisect,
        (jnp.zeros((t_new, 1), jnp.int32), jnp.full((t_new, 1), float(past + LANES), f32)))
    thr = _key_to_threshold(tu)

    k_pos = lax.broadcasted_iota(jnp.int32, sc.shape, 1)
    q_pos = past + lax.broadcasted_iota(jnp.int32, sc.shape, 0)
    nbits = (past + LANES - 1).bit_length()

    def tie_limit():
        need = float(k_sel) - count(sc > thr)
        def index_bit(it, p):
            cand = p + jnp.left_shift(jnp.int32(1), nbits - 1 - it)
            below = count((sc == thr) & (k_pos < cand))
            return jnp.where(below < need, cand, p)
        return lax.fori_loop(0, nbits, index_bit, jnp.zeros((t_new, 1), jnp.int32))

    j_last = lax.cond(jnp.max(cnt_ge) > float(k_sel), tie_limit,
                      lambda: jnp.full((t_new, 1), past + LANES, jnp.int32))
    sel = ((sc > thr) | ((sc == thr) & (k_pos <= j_last))) & (k_pos <= q_pos)
    dist = jnp.where(sel, (k_pos - q_pos).astype(f32), MASKED_DIST)

    def logits(s, d):
        return s + slope * jnp.concatenate([d] * N_HEADS, axis=0)

    m = jnp.full((n_rows, 1), -jnp.inf, f32)
    for c in range(past // kc):
        cols = slice(c * kc, (c + 1) * kc)
        s = logits(_dot(q_all, _mxu(kbuf[:, cols])), dist[:, cols])
        s_s[:, cols] = s
        m = jnp.maximum(m, jnp.max(s, axis=1, keepdims=True))
    s = logits(_dot_nt(q_all, k_new), dist[:, past:past + LANES])
    s_s[:, past:past + LANES] = s
    m = jnp.maximum(m, jnp.max(s, axis=1, keepdims=True))

    l = jnp.zeros((n_rows, 1), f32)
    o = jnp.zeros((n_rows, LANES), f32)
    for c in range(past // kc):
        cols = slice(c * kc, (c + 1) * kc)
        p = jnp.exp(s_s[:, cols] - m)
        l = l + jnp.sum(p, axis=1, keepdims=True)
        o = o + _dot_nt(_mxu(p), _mxu(vbuf[:, cols]))
    p = jnp.exp(s_s[:, past:past + LANES] - m)
    l = l + jnp.sum(p, axis=1, keepdims=True)
    o = (o + _dot(_mxu(p), v_new)) / l

    for j in range(N_HEADS // 2):
        o_ref[0, :, j * LANES:(j + 1) * LANES] = _pack_head_pair(
            o[(2 * j) * t_new:(2 * j + 1) * t_new], o[(2 * j + 1) * t_new:(2 * j + 2) * t_new],
            (2 * j) // GROUP)


def _attn_sample(page_table, q, iq, ikw, k, v, cache_kt, cache_vt, cache_kit, *, kc):
    db, t_new, _ = q.shape
    n_pages = page_table.shape[1]
    past = n_pages * PAGE_SIZE
    k_sel = min(TOPK_MAX, (past + t_new) // 4)
    blk = lambda width: pl.BlockSpec((1, t_new, width), lambda i, pt: (i, 0, 0))
    hbm = pl.BlockSpec(memory_space=pl.ANY)
    f32 = jnp.float32
    return pl.pallas_call(
        functools.partial(_attn_sample_kernel, n_pages=n_pages, t_new=t_new, kc=kc, k_sel=k_sel),
        grid_spec=pltpu.PrefetchScalarGridSpec(
            num_scalar_prefetch=1,
            grid=(db,),
            in_specs=[blk(D_ATT), blk(N_IDX_HEADS * IDX_DIM), blk(LANES), blk(D_KV), blk(D_KV),
                      hbm, hbm, hbm],
            out_specs=blk(D_ATT),
            scratch_shapes=[
                pltpu.VMEM((D_KV, past), f32),
                pltpu.VMEM((D_KV, past), f32),
                pltpu.VMEM((IDX_DIM, past), f32),
                pltpu.VMEM((t_new, past + LANES), f32),
                pltpu.VMEM((N_HEADS * t_new, past + LANES), f32),
                pltpu.SemaphoreType.DMA((3,)),
            ]),
        out_shape=jax.ShapeDtypeStruct((db, t_new, D_ATT), f32),
        compiler_params=pltpu.CompilerParams(
            dimension_semantics=("arbitrary",), vmem_limit_bytes=VMEM_LIMIT),
        name="attn_sample",
    )(page_table, q, iq, ikw, k, v, cache_kt, cache_vt, cache_kit)


def _prepare_weights(norm_gain, w_in, conv_w, w_up_conv, w_up_attn, w_out, final_gain):
    split = C_IKW + IDX_DIM + N_IDX_HEADS
    pad = jnp.zeros((D_MODEL, D_IN_PAD - D_IN_RAW), w_in.dtype)
    w_pad = _mxu(jnp.concatenate([w_in[:, :split], pad, w_in[:, split:]], axis=1))
    wt_keys = _mxu(jnp.concatenate(
        [w_in[:, C_K:C_K + 2 * D_KV], w_in[:, C_IKW:C_IKW + IDX_DIM]], axis=1).T)
    return (norm_gain, w_pad, wt_keys, conv_w), (_mxu(w_up_conv), _mxu(w_up_attn), _mxu(w_out), final_gain)


def _prompt_layer(x, proj_w, merge_w, *, tm_proj, tm_merge, tq, ck):
    b, t, _ = x.shape
    prev = jnp.zeros((b, CONV_WIDTH - 1, D_CONV), x.dtype)
    yc, q, az, iq, ikw, gg, cs, kt, vt, kit = _project(
        x, prev, *proj_w, tm=tm_proj, qdtype=MXU_DTYPE, keys_on_lanes=True)
    attn = _attn_prompt(q, iq, ikw, kt, vt, kit, tq=tq, ck=ck)
    flat = lambda a: a.reshape(b * t, a.shape[-1])
    y = _merge(flat(x), flat(yc), flat(attn), flat(az), flat(gg), *merge_w, tm=tm_merge)
    heads = lambda a: a.reshape(b, N_KV_HEADS, HEAD_DIM, t).transpose(0, 3, 1, 2)[None]
    return y.reshape(b, t, D_MODEL), heads(kt), heads(vt), kit.transpose(0, 2, 1)[None], cs[None]


def _sample_layer(x, prev, page_table, cache_k, cache_v, cache_kidx, proj_w, merge_w, *, tm_merge, kc):
    b, t, _ = x.shape
    n_phys = cache_k.shape[0]
    yc, q, az, iq, ikw, gg, cs, k, v = _project(
        x, prev, *proj_w, tm=t, qdtype=jnp.float32, keys_on_lanes=False)
    cache_kt = cache_k.transpose(0, 2, 3, 1).reshape(n_phys, D_KV, PAGE_SIZE)
    cache_vt = cache_v.transpose(0, 2, 3, 1).reshape(n_phys, D_KV, PAGE_SIZE)
    cache_kit = cache_kidx.transpose(0, 2, 1)
    attn = _attn_sample(page_table, q, iq, ikw, k, v, cache_kt, cache_vt, cache_kit, kc=kc)
    flat = lambda a: a.reshape(b * t, a.shape[-1])
    y = _merge(flat(x), flat(yc), flat(attn), flat(az), flat(gg), *merge_w, tm=tm_merge)
    heads = lambda a: a.reshape(1, b, t, N_KV_HEADS, HEAD_DIM)
    return y.reshape(b, t, D_MODEL), heads(k), heads(v), ikw[None, :, :, :IDX_DIM], cs[None]


def kernel(x_prompt, x_sample, cache_k, cache_v, cache_kidx, state_conv, page_table, norm_gain, w_in,
           conv_w, w_up_conv, w_up_attn, w_out, final_gain):
    assert norm_gain.shape[0] == 1, "single-layer step"
    proj_w, merge_w = _prepare_weights(norm_gain[0], w_in[0], conv_w[0], w_up_conv[0], w_up_attn[0],
                                       w_out[0], final_gain)
    yp, kp, vp, ip, cp = _prompt_layer(x_prompt, proj_w, merge_w, tm_proj=256, tm_merge=512, tq=128, ck=512)
    ys, ks, vs, is_, cs = _sample_layer(x_sample, state_conv[0], page_table, cache_k[0], cache_v[0],
                                        cache_kidx[0], proj_w, merge_w, tm_merge=512, kc=1024)
    return (yp, ys, kp, vp, ip, cp, ks, vs, is_, cs)
```

```python
import functools

import jax
import jax.numpy as jnp
from jax import lax
from jax.experimental import pallas as pl
from jax.experimental.pallas import tpu as pltpu

D_MODEL = 1024
D_CONV = 512
CONV_WIDTH = 3
N_HEADS = 8
HEAD_DIM = 64
N_KV_HEADS = 2
GROUP = N_HEADS // N_KV_HEADS
D_ATT = N_HEADS * HEAD_DIM
D_KV = N_KV_HEADS * HEAD_DIM
N_IDX_HEADS = 8
IDX_DIM = 64
D_IQ = N_IDX_HEADS * IDX_DIM
TOPK_MAX = 256
PAGE_SIZE = 128
EPS = 1e-6

LANES = 128
SUBLANES = 8
VMEM_LIMIT = 48 * 1024 * 1024
MXU_DTYPE = jnp.bfloat16

C_CONV = 0
C_Q = 4 * D_CONV
C_K = C_Q + D_ATT
C_V = C_K + D_KV
C_AZ = C_V + D_KV
C_IQ = C_AZ + D_ATT
C_IKW = C_IQ + D_IQ
C_G = C_IKW + LANES
D_IN_RAW = C_IKW + IDX_DIM + N_IDX_HEADS + 2 * D_MODEL
D_IN_PAD = C_G + 2 * D_MODEL
R_Q = 0
R_IQ = R_Q + D_ATT
R_K = R_IQ + D_IQ
R_V = R_K + D_KV
R_IKW = R_V + D_KV
D_T = R_IKW + IDX_DIM + N_IDX_HEADS

MASKED_DIST = -1e33
MASKED_LOGIT = -1e30
LOG2E = 1.4426950408889634
N_LOG2E_PIECES = 4
POS_RADIX = 64
INT_MIN = -2 ** 31
KEY_NEG_INF = INT_MIN + 0x7FFFFF


def _dot(a, b):
    return jnp.dot(a, b, preferred_element_type=jnp.float32)


def _dot_nt(a, b):
    return lax.dot_general(a, b, (((1,), (1,)), ((), ())), preferred_element_type=jnp.float32)


def _mxu(x):
    return x.astype(MXU_DTYPE)


def _proj_kernel(x_ref, prev_ref, g_ref, w_ref, wt_ref, cw_ref, *refs, tm, tokens_on_lanes):
    yc_ref, az_ref, gg_ref, cs_ref = refs[:4]
    u_buf = refs[-1]
    t = pl.program_id(1)
    nt = pl.num_programs(1)
    x = x_ref[0]
    xn = x * lax.rsqrt(jnp.mean(x * x, axis=-1, keepdims=True) + EPS) * g_ref[...]
    xn = _mxu(xn)

    def proj(c0, width):
        return _dot(xn, w_ref[:, c0:c0 + width])

    def proj_t(r0, height):
        return _dot_nt(wt_ref[r0:r0 + height, :], xn)

    @pl.when(t == 0)
    def _():
        u_buf[0:SUBLANES, :] = jnp.zeros((SUBLANES, D_CONV), jnp.float32)
        u_buf[SUBLANES - 2:SUBLANES, :] = prev_ref[0]

    @pl.when(t > 0)
    def _():
        u_buf[0:SUBLANES, :] = u_buf[tm:tm + SUBLANES, :]

    u_buf[SUBLANES:SUBLANES + tm, :] = proj(C_CONV + D_CONV, D_CONV) * proj(C_CONV + 2 * D_CONV, D_CONV)
    conv = (cw_ref[0:1, :] * u_buf[SUBLANES - 2:SUBLANES - 2 + tm, :]
            + cw_ref[1:2, :] * u_buf[SUBLANES - 1:SUBLANES - 1 + tm, :]
            + cw_ref[2:3, :] * u_buf[SUBLANES:SUBLANES + tm, :])
    yc_ref[0] = (proj(C_CONV, D_CONV) * conv * jax.nn.silu(proj(C_CONV + 3 * D_CONV, D_CONV))).astype(yc_ref.dtype)

    @pl.when(t == nt - 1)
    def _():
        cs_ref[0] = u_buf[tm + SUBLANES - 2:tm + SUBLANES, :]

    az_ref[0] = proj(C_AZ, D_ATT)
    gg_ref[0] = proj(C_G, 2 * D_MODEL)
    if tokens_on_lanes:
        qt_ref, iqt_ref, kt_ref, vt_ref, kit_ref, iwt_ref, kr_ref, ikwr_ref, vtc_ref = refs[4:13]
        qt_ref[0] = _mxu(proj_t(R_Q, D_ATT) * (HEAD_DIM ** -0.5 * LOG2E))
        iqt_ref[0] = _mxu(proj_t(R_IQ, D_IQ) * (IDX_DIM ** -0.5))
        kt_ref[0] = proj_t(R_K, D_KV)
        vt = proj_t(R_V, D_KV)
        vt_ref[0] = vt
        for s in range(tm // LANES):
            vtc_ref[0, s] = _mxu(vt[:, s * LANES:(s + 1) * LANES])
        kit_ref[0] = proj_t(R_IKW, IDX_DIM)
        iwt_ref[0] = proj_t(R_IKW + IDX_DIM, N_IDX_HEADS)
        kr_ref[0, :, 0:D_KV] = _mxu(proj(C_K, D_KV))
        pos = t * tm + lax.broadcasted_iota(jnp.int32, (tm, LANES), 0)
        lane = lax.broadcasted_iota(jnp.int32, (tm, LANES), 1)
        feat = jnp.where(lane % 2 == 0, pos // POS_RADIX, pos % POS_RADIX)
        kr_ref[0, :, D_KV:D_KV + LANES] = _mxu(
            jnp.where(lane < 2 * N_LOG2E_PIECES, feat, 0).astype(jnp.float32))
        ikwr_ref[0] = _mxu(proj(C_IKW, LANES))
    else:
        q_ref, iq_ref, ikw_ref, k_ref, v_ref = refs[4:9]
        q_ref[0] = proj(C_Q, D_ATT) * (HEAD_DIM ** -0.5)
        iq_ref[0] = proj(C_IQ, D_IQ) * (IDX_DIM ** -0.5)
        ikw_ref[0] = proj(C_IKW, LANES)
        k_ref[0] = proj(C_K, D_KV)
        v_ref[0] = proj(C_V, D_KV)


def _project(x, prev, norm_gain, w_pad, w_t, conv_w, *, tm, tokens_on_lanes):
    b, t, _ = x.shape
    nt = t // tm
    row = lambda width: pl.BlockSpec((1, tm, width), lambda i, j: (i, j, 0))
    col = lambda height: pl.BlockSpec((1, height, tm), lambda i, j: (i, 0, j))
    full = lambda shape: pl.BlockSpec(shape, lambda i, j: (0,) * len(shape))
    state = pl.BlockSpec((1, CONV_WIDTH - 1, D_CONV), lambda i, j: (i, 0, 0))
    f32 = jnp.float32
    sds = jax.ShapeDtypeStruct
    out_shapes = [sds((b, t, D_CONV), MXU_DTYPE), sds((b, t, D_ATT), f32), sds((b, t, 2 * D_MODEL), f32),
                  sds((b, CONV_WIDTH - 1, D_CONV), f32)]
    out_specs = [row(D_CONV), row(D_ATT), row(2 * D_MODEL), state]
    if tokens_on_lanes:
        out_shapes += [sds((b, D_ATT, t), MXU_DTYPE), sds((b, D_IQ, t), MXU_DTYPE),
                       sds((b, D_KV, t), f32), sds((b, D_KV, t), f32), sds((b, IDX_DIM, t), f32),
                       sds((b, N_IDX_HEADS, t), f32),
                       sds((b, t, D_KV + LANES), MXU_DTYPE), sds((b, t, LANES), MXU_DTYPE),
                       sds((b, t // LANES, D_KV, LANES), MXU_DTYPE)]
        out_specs += [col(D_ATT), col(D_IQ), col(D_KV), col(D_KV), col(IDX_DIM), col(N_IDX_HEADS),
                      row(D_KV + LANES), row(LANES),
                      pl.BlockSpec((1, tm // LANES, D_KV, LANES), lambda i, j: (i, j, 0, 0))]
    else:
        out_shapes += [sds((b, t, D_ATT), f32), sds((b, t, D_IQ), f32), sds((b, t, LANES), f32),
                       sds((b, t, D_KV), f32), sds((b, t, D_KV), f32)]
        out_specs += [row(D_ATT), row(D_IQ), row(LANES), row(D_KV), row(D_KV)]
    return pl.pallas_call(
        functools.partial(_proj_kernel, tm=tm, tokens_on_lanes=tokens_on_lanes),
        grid=(b, nt),
        in_specs=[row(D_MODEL), state, full((1, D_MODEL)), full((D_MODEL, D_IN_PAD)),
                  full((D_T, D_MODEL)), full((CONV_WIDTH, D_CONV))],
        out_specs=out_specs,
        out_shape=out_shapes,
        scratch_shapes=[pltpu.VMEM((tm + SUBLANES, D_CONV), f32)],
        compiler_params=pltpu.CompilerParams(
            dimension_semantics=("arbitrary", "arbitrary"), vmem_limit_bytes=VMEM_LIMIT),
        name="proj",
    )(x, prev, norm_gain.reshape(1, D_MODEL), w_pad, w_t, conv_w)


def _merge_kernel(x_ref, yc_ref, at_ref, az_ref, gg_ref, wuc_ref, wua_ref, wo_ref, fg_ref, y_ref):
    ya = _mxu(at_ref[...] * jax.nn.silu(az_ref[...]))
    gc = gg_ref[:, 0:D_MODEL]
    ga = gg_ref[:, D_MODEL:2 * D_MODEL]
    m = (jax.nn.sigmoid(gc) * _dot(yc_ref[...], wuc_ref[...])
         + jax.nn.sigmoid(ga) * _dot(ya, wua_ref[...]))
    xo = x_ref[...] + _dot(_mxu(m), wo_ref[...])
    y_ref[...] = xo * lax.rsqrt(jnp.mean(xo * xo, axis=-1, keepdims=True) + EPS) * fg_ref[...]


def _merge(x, yc, attn, az, gg, wuc, wua, wo, final_gain, *, tm):
    n = x.shape[0]
    row = lambda width: pl.BlockSpec((tm, width), lambda i: (i, 0))
    full = lambda shape: pl.BlockSpec(shape, lambda i: (0,) * len(shape))
    return pl.pallas_call(
        _merge_kernel,
        grid=(n // tm,),
        in_specs=[row(D_MODEL), row(D_CONV), row(D_ATT), row(D_ATT), row(2 * D_MODEL),
                  full((D_CONV, D_MODEL)), full((D_ATT, D_MODEL)), full((D_MODEL, D_MODEL)),
                  full((1, D_MODEL))],
        out_specs=row(D_MODEL),
        out_shape=jax.ShapeDtypeStruct((n, D_MODEL), jnp.float32),
        compiler_params=pltpu.CompilerParams(
            dimension_semantics=("arbitrary",), vmem_limit_bytes=VMEM_LIMIT),
        name="merge",
    )(x, yc, attn, az, gg, wuc, wua, wo, final_gain.reshape(1, D_MODEL))


def _key_to_threshold(tu):
    key = tu ^ jnp.int32(INT_MIN)
    bits = key ^ ((key >> 31) & jnp.int32(0x7FFFFFFF))
    thr = lax.bitcast_convert_type(bits, jnp.float32)
    return jnp.where(key < jnp.int32(KEY_NEG_INF), -jnp.inf, thr)


def _lane_column(x, lane):
    ids = lax.broadcasted_iota(jnp.int32, x.shape, 1)
    return jnp.sum(jnp.where(ids == lane, x, 0.0), axis=1, keepdims=True)


def _swap_halves(x):
    return pltpu.roll(x, LANES // 2, axis=1)


def _head_slope(h):
    return 2.0 ** (-8.0 * (h + 1) / N_HEADS)


def _pack_head_pair(a0, a1, kv_head):
    lo = lax.broadcasted_iota(jnp.int32, a0.shape, 1) < LANES // 2
    if kv_head == 0:
        return jnp.where(lo, a0, _swap_halves(a1))
    return jnp.where(lo, _swap_halves(a0), a1)


N_COUNT_ACC = 4
KEY_BLOCK = LANES


def _alibi_rows(tq):
    pieces, rest = [], jnp.float32(LOG2E)
    for _ in range(N_LOG2E_PIECES):
        piece = rest.astype(MXU_DTYPE).astype(jnp.float32)
        pieces.append(piece)
        rest = rest - piece
    rows = jnp.zeros((N_HEADS, LANES), jnp.float32)
    for h in range(N_HEADS):
        for j, piece in enumerate(pieces):
            rows = rows.at[h, 2 * j].set(_head_slope(h) * POS_RADIX * piece)
            rows = rows.at[h, 2 * j + 1].set(_head_slope(h) * piece)
    return _mxu(jnp.broadcast_to(rows[:, :, None], (N_HEADS, LANES, tq)))


def _attn_prompt_kernel(qt_ref, iqt_ref, iwt_ref, kr_ref, ikwr_ref, vtc_ref, aw_ref, o_ref,
                        qw_s, iqw_s, sc_s, mask_s, m_s, l_s, a_s, acc_s, j_s, ot_s, s_s, p_s,
                        *, seq, tq, cks, k_sel):
    i = pl.program_id(1)
    n_kb = (i + 1) * (tq // KEY_BLOCK)
    n_cks = ((i + 1) * tq + cks - 1) // cks
    f32 = jnp.float32

    zeros = jnp.zeros((HEAD_DIM, tq), MXU_DTYPE)
    for h in range(N_HEADS):
        g = h // GROUP
        qw_s[h, g * HEAD_DIM:(g + 1) * HEAD_DIM, :] = qt_ref[0, h * HEAD_DIM:(h + 1) * HEAD_DIM, :]
        qw_s[h, (1 - g) * HEAD_DIM:(2 - g) * HEAD_DIM, :] = zeros
        qw_s[h, D_KV:D_KV + LANES, :] = aw_ref[h]
        iqw_s[h, 0:IDX_DIM, :] = iqt_ref[0, h * IDX_DIM:(h + 1) * IDX_DIM, :]
        iqw_s[h, IDX_DIM:2 * IDX_DIM, :] = zeros

    w_rows = iwt_ref[0] * (N_IDX_HEADS ** -0.5)
    q_pos_c = i * tq + lax.broadcasted_iota(jnp.int32, (cks, tq), 1)
    k_row_c = lax.broadcasted_iota(jnp.int32, (cks, tq), 0)

    def score_chunk(c, carry):
        kk = ikwr_ref[0, pl.ds(pl.multiple_of(c * cks, cks), cks), :]
        acc = jnp.zeros((cks, tq), f32)
        for h in range(N_IDX_HEADS):
            acc = acc + jnp.maximum(_dot(kk, iqw_s[h]), 0.0) * w_rows[h:h + 1, :]
        acc = jnp.where(c * cks + k_row_c <= q_pos_c, acc, -jnp.inf)
        for t in range(cks // KEY_BLOCK):
            sc_s[c * (cks // KEY_BLOCK) + t] = acc[t * KEY_BLOCK:(t + 1) * KEY_BLOCK]
        return carry
    lax.fori_loop(0, n_cks, score_chunk, 0)

    sub_id = lax.broadcasted_iota(jnp.int32, (SUBLANES, tq), 0)

    def count(pred):
        def body(j, parts):
            parts = list(parts)
            for r in range(KEY_BLOCK // SUBLANES):
                rows = sc_s[j, r * SUBLANES:(r + 1) * SUBLANES, :]
                hit = pred(rows, j * KEY_BLOCK + r * SUBLANES + sub_id)
                parts[r % N_COUNT_ACC] = parts[r % N_COUNT_ACC] + jnp.where(hit, 1.0, 0.0)
            return tuple(parts)
        parts = lax.fori_loop(0, n_kb, body, (jnp.zeros((SUBLANES, tq), f32),) * N_COUNT_ACC)
        return jnp.sum(functools.reduce(lambda a, b: a + b, parts), axis=0, keepdims=True)

    def rows8(row):
        return jnp.broadcast_to(row, (SUBLANES, tq))

    def bisect(it, carry):
        tu, cnt_acc = carry
        cand = tu | jnp.left_shift(jnp.int32(1), 31 - it)
        thr_c = rows8(_key_to_threshold(cand))
        cnt = count(lambda s, kp: s >= thr_c)
        ok = cnt >= float(k_sel)
        return jnp.where(ok, cand, tu), jnp.where(ok, cnt, cnt_acc)

    n_all = (n_kb * KEY_BLOCK).astype(f32)
    tu, cnt_ge = lax.fori_loop(
        0, 32, bisect, (jnp.zeros((1, tq), jnp.int32), jnp.zeros((1, tq), f32) + n_all))
    thr = _key_to_threshold(tu)

    j_s[...] = jnp.full((1, tq), seq, jnp.int32)

    @pl.when(jnp.max(cnt_ge) > float(k_sel))
    def _():
        thr8 = rows8(thr)
        need = float(k_sel) - count(lambda s, kp: s > thr8)
        nbits = (seq - 1).bit_length()
        def index_bit(it, p):
            cand = p + jnp.left_shift(jnp.int32(1), nbits - 1 - it)
            cand8 = rows8(cand)
            below = count(lambda s, kp: (s == thr8) & (kp < cand8))
            return jnp.where(below < need, cand, p)
        j_s[...] = lax.fori_loop(0, nbits, index_bit, jnp.zeros((1, tq), jnp.int32))

    m_s[...] = jnp.full(m_s.shape, -jnp.inf, f32)
    l_s[...] = jnp.zeros(l_s.shape, f32)
    acc_s[...] = jnp.zeros(acc_s.shape, f32)
    j_last = j_s[...]
    q_pos = i * tq + lax.broadcasted_iota(jnp.int32, (KEY_BLOCK, tq), 1)
    k_row = lax.broadcasted_iota(jnp.int32, (KEY_BLOCK, tq), 0)

    def attend_block(j, carry):
        k_pos = j * KEY_BLOCK + k_row
        s_idx = sc_s[j]
        sel = ((s_idx > thr) | ((s_idx == thr) & (k_pos <= j_last))) & (k_pos <= q_pos)
        mask_s[...] = jnp.where(sel, 0.0, MASKED_LOGIT)
        kk = kr_ref[0, pl.ds(pl.multiple_of(j * KEY_BLOCK, KEY_BLOCK), KEY_BLOCK), :]
        vt = vtc_ref[0, j]
        for h in range(N_HEADS):
            s_s[h] = _dot(kk, qw_s[h]) + mask_s[...]
        for h in range(N_HEADS):
            m_old = m_s[h:h + 1, :]
            m_new = jnp.maximum(m_old, jnp.max(s_s[h], axis=0, keepdims=True))
            alpha = jnp.exp2(m_old - m_new)
            p = jnp.exp2(s_s[h] - m_new)
            l_s[h:h + 1, :] = alpha * l_s[h:h + 1, :] + jnp.sum(p, axis=0, keepdims=True)
            p_s[h] = _mxu(p)
            m_s[h:h + 1, :] = m_new
            a_s[h:h + 1, :] = alpha
        for h in range(N_HEADS):
            g = h // GROUP
            acc_s[h] = a_s[h:h + 1, :] * acc_s[h] + _dot(vt[g * HEAD_DIM:(g + 1) * HEAD_DIM, :], p_s[h])
        return carry
    lax.fori_loop(0, n_kb, attend_block, 0)

    for h in range(N_HEADS):
        ot_s[h * HEAD_DIM:(h + 1) * HEAD_DIM, :] = acc_s[h] / l_s[h:h + 1, :]
    o_ref[0] = ot_s[...].T


def _attn_prompt(qt, iqt, iwt, k_rows, ikw_rows, vt_blocks, *, tq, cks):
    b, _, seq = qt.shape
    n_kb = seq // KEY_BLOCK
    k_sel = min(TOPK_MAX, seq // 4)
    colblk = lambda height: pl.BlockSpec((1, height, tq), lambda i, j: (i, 0, j))
    whole = lambda shape: pl.BlockSpec((1,) + shape, lambda i, j: (i,) + (0,) * len(shape))
    f32 = jnp.float32
    return pl.pallas_call(
        functools.partial(_attn_prompt_kernel, seq=seq, tq=tq, cks=cks, k_sel=k_sel),
        grid=(b, seq // tq),
        in_specs=[colblk(D_ATT), colblk(D_IQ), colblk(N_IDX_HEADS),
                  whole((seq, D_KV + LANES)), whole((seq, LANES)), whole((n_kb, D_KV, KEY_BLOCK)),
                  pl.BlockSpec((N_HEADS, LANES, tq), lambda i, j: (0, 0, 0))],
        out_specs=pl.BlockSpec((1, tq, D_ATT), lambda i, j: (i, j, 0)),
        out_shape=jax.ShapeDtypeStruct((b, seq, D_ATT), f32),
        scratch_shapes=[
            pltpu.VMEM((N_HEADS, D_KV + LANES, tq), MXU_DTYPE),
            pltpu.VMEM((N_IDX_HEADS, LANES, tq), MXU_DTYPE),
            pltpu.VMEM((n_kb, KEY_BLOCK, tq), f32),
            pltpu.VMEM((KEY_BLOCK, tq), f32),
            pltpu.VMEM((N_HEADS, tq), f32),
            pltpu.VMEM((N_HEADS, tq), f32),
            pltpu.VMEM((N_HEADS, tq), f32),
            pltpu.VMEM((N_HEADS, HEAD_DIM, tq), f32),
            pltpu.VMEM((1, tq), jnp.int32),
            pltpu.VMEM((D_ATT, tq), f32),
            pltpu.VMEM((N_HEADS, KEY_BLOCK, tq), f32),
            pltpu.VMEM((N_HEADS, KEY_BLOCK, tq), MXU_DTYPE),
        ],
        compiler_params=pltpu.CompilerParams(
            dimension_semantics=("arbitrary", "arbitrary"), vmem_limit_bytes=VMEM_LIMIT),
        name="attn_prompt",
    )(qt, iqt, iwt, k_rows, ikw_rows, vt_blocks, _alibi_rows(tq))


def _attn_sample_kernel(pt_ref, q_ref, iq_ref, ikw_ref, k_ref, v_ref, ck_hbm, cv_hbm, ci_hbm, o_ref,
                        kbuf, vbuf, ibuf, sc_s, s_s, sem,
                        *, n_pages, t_new, kc, k_sel):
    b = pl.program_id(0)
    past = n_pages * PAGE_SIZE
    n_rows = N_HEADS * t_new
    f32 = jnp.float32

    def page_copies(p):
        phys = pt_ref[b, p]
        cols = pl.ds(pl.multiple_of(p * PAGE_SIZE, PAGE_SIZE), PAGE_SIZE)
        return (pltpu.make_async_copy(ck_hbm.at[phys], kbuf.at[:, cols], sem.at[0]),
                pltpu.make_async_copy(cv_hbm.at[phys], vbuf.at[:, cols], sem.at[1]),
                pltpu.make_async_copy(ci_hbm.at[phys], ibuf.at[:, cols], sem.at[2]))

    def start_page(p, carry):
        for cp in page_copies(p):
            cp.start()
        return carry
    lax.fori_loop(0, n_pages, start_page, 0)

    lo = lax.broadcasted_iota(jnp.int32, (t_new, LANES), 1) < LANES // 2
    ikw = ikw_ref[0]
    iq_rows, q_rows, w_rows = [], [], []
    for h in range(N_HEADS):
        it = iq_ref[0, :, (h // 2) * LANES:(h // 2 + 1) * LANES]
        iq_rows.append((it if h % 2 == 0 else _swap_halves(it))[:, 0:IDX_DIM])
        qt = q_ref[0, :, (h // 2) * LANES:(h // 2 + 1) * LANES]
        g, par = h // GROUP, h % 2
        if g != par:
            qt = _swap_halves(qt)
        q_rows.append(jnp.where(lo, qt, 0.0) if g == 0 else jnp.where(lo, 0.0, qt))
        w_rows.append(_lane_column(ikw, IDX_DIM + h) * (N_IDX_HEADS ** -0.5))
    iq_all = _mxu(jnp.concatenate(iq_rows, axis=0))
    q_all = _mxu(jnp.concatenate(q_rows, axis=0))
    w_all = jnp.concatenate(w_rows, axis=0)
    slope = jnp.concatenate(
        [jnp.full((t_new, 1), _head_slope(h), f32) for h in range(N_HEADS)], axis=0)

    pad = jnp.zeros((LANES - t_new, LANES), f32)
    ik_new = _mxu(jnp.concatenate([ikw, pad], axis=0)[:, 0:IDX_DIM])
    k_new = _mxu(jnp.concatenate([k_ref[0], pad], axis=0))
    v_new = _mxu(jnp.concatenate([v_ref[0], pad], axis=0))

    def head_sum(r):
        acc = r[0:t_new]
        for h in range(1, N_HEADS):
            acc = acc + r[h * t_new:(h + 1) * t_new]
        return acc

    def wait_page(p, carry):
        for cp in page_copies(p):
            cp.wait()
        return carry
    lax.fori_loop(0, n_pages, wait_page, 0)

    for c in range(past // kc):
        cols = slice(c * kc, (c + 1) * kc)
        sc_s[:, cols] = head_sum(jnp.maximum(_dot(iq_all, _mxu(ibuf[:, cols])), 0.0) * w_all)
    t_id = lax.broadcasted_iota(jnp.int32, (t_new, LANES), 0)
    j_id = lax.broadcasted_iota(jnp.int32, (t_new, LANES), 1)
    s_new = head_sum(jnp.maximum(_dot_nt(iq_all, ik_new), 0.0) * w_all)
    sc_s[:, past:past + LANES] = jnp.where(j_id <= t_id, s_new, -jnp.inf)

    sc = sc_s[...]
    def count(mask):
        return jnp.sum(jnp.where(mask, 1.0, 0.0), axis=1, keepdims=True)

    def bisect(it, carry):
        tu, cnt_acc = carry
        cand = tu | jnp.left_shift(jnp.int32(1), 31 - it)
        cnt = count(sc >= _key_to_threshold(cand))
        ok = cnt >= float(k_sel)
        return jnp.where(ok, cand, tu), jnp.where(ok, cnt, cnt_acc)

    tu, cnt_ge = lax.fori_loop(
        0, 32, bisect,
        (jnp.zeros((t_new, 1), jnp.int32), jnp.full((t_new, 1), float(past + LANES), f32)))
    thr = _key_to_threshold(tu)

    k_pos = lax.broadcasted_iota(jnp.int32, sc.shape, 1)
    q_pos = past + lax.broadcasted_iota(jnp.int32, sc.shape, 0)
    nbits = (past + LANES - 1).bit_length()

    def tie_limit():
        need = float(k_sel) - count(sc > thr)
        def index_bit(it, p):
            cand = p + jnp.left_shift(jnp.int32(1), nbits - 1 - it)
            below = count((sc == thr) & (k_pos < cand))
            return jnp.where(below < need, cand, p)
        return lax.fori_loop(0, nbits, index_bit, jnp.zeros((t_new, 1), jnp.int32))

    j_last = lax.cond(jnp.max(cnt_ge) > float(k_sel), tie_limit,
                      lambda: jnp.full((t_new, 1), past + LANES, jnp.int32))
    sel = ((sc > thr) | ((sc == thr) & (k_pos <= j_last))) & (k_pos <= q_pos)
    dist = jnp.where(sel, (k_pos - q_pos).astype(f32), MASKED_DIST)

    def logits(s, d):
        return s + slope * jnp.concatenate([d] * N_HEADS, axis=0)

    m = jnp.full((n_rows, 1), -jnp.inf, f32)
    for c in range(past // kc):
        cols = slice(c * kc, (c + 1) * kc)
        s = logits(_dot(q_all, _mxu(kbuf[:, cols])), dist[:, cols])
        s_s[:, cols] = s
        m = jnp.maximum(m, jnp.max(s, axis=1, keepdims=True))
    s = logits(_dot_nt(q_all, k_new), dist[:, past:past + LANES])
    s_s[:, past:past + LANES] = s
    m = jnp.maximum(m, jnp.max(s, axis=1, keepdims=True))

    l = jnp.zeros((n_rows, 1), f32)
    o = jnp.zeros((n_rows, LANES), f32)
    for c in range(past // kc):
        cols = slice(c * kc, (c + 1) * kc)
        p = jnp.exp(s_s[:, cols] - m)
        l = l + jnp.sum(p, axis=1, keepdims=True)
        o = o + _dot_nt(_mxu(p), _mxu(vbuf[:, cols]))
    p = jnp.exp(s_s[:, past:past + LANES] - m)
    l = l + jnp.sum(p, axis=1, keepdims=True)
    o = (o + _dot(_mxu(p), v_new)) / l

    for j in range(N_HEADS // 2):
        o_ref[0, :, j * LANES:(j + 1) * LANES] = _pack_head_pair(
            o[(2 * j) * t_new:(2 * j + 1) * t_new], o[(2 * j + 1) * t_new:(2 * j + 2) * t_new],
            (2 * j) // GROUP)


def _attn_sample(page_table, q, iq, ikw, k, v, cache_kt, cache_vt, cache_kit, *, kc):
    db, t_new, _ = q.shape
    n_pages = page_table.shape[1]
    past = n_pages * PAGE_SIZE
    k_sel = min(TOPK_MAX, (past + t_new) // 4)
    blk = lambda width: pl.BlockSpec((1, t_new, width), lambda i, pt: (i, 0, 0))
    hbm = pl.BlockSpec(memory_space=pl.ANY)
    f32 = jnp.float32
    return pl.pallas_call(
        functools.partial(_attn_sample_kernel, n_pages=n_pages, t_new=t_new, kc=kc, k_sel=k_sel),
        grid_spec=pltpu.PrefetchScalarGridSpec(
            num_scalar_prefetch=1,
            grid=(db,),
            in_specs=[blk(D_ATT), blk(D_IQ), blk(LANES), blk(D_KV), blk(D_KV), hbm, hbm, hbm],
            out_specs=blk(D_ATT),
            scratch_shapes=[
                pltpu.VMEM((D_KV, past), f32),
                pltpu.VMEM((D_KV, past), f32),
                pltpu.VMEM((IDX_DIM, past), f32),
                pltpu.VMEM((t_new, past + LANES), f32),
                pltpu.VMEM((N_HEADS * t_new, past + LANES), f32),
                pltpu.SemaphoreType.DMA((3,)),
            ]),
        out_shape=jax.ShapeDtypeStruct((db, t_new, D_ATT), f32),
        compiler_params=pltpu.CompilerParams(
            dimension_semantics=("arbitrary",), vmem_limit_bytes=VMEM_LIMIT),
        name="attn_sample",
    )(page_table, q, iq, ikw, k, v, cache_kt, cache_vt, cache_kit)


def _prepare_weights(norm_gain, w_in, conv_w, w_up_conv, w_up_attn, w_out, final_gain):
    split = C_IKW + IDX_DIM + N_IDX_HEADS
    pad = jnp.zeros((D_MODEL, D_IN_PAD - D_IN_RAW), w_in.dtype)
    w_pad = _mxu(jnp.concatenate([w_in[:, :split], pad, w_in[:, split:]], axis=1))
    w_t = _mxu(jnp.concatenate(
        [w_in[:, C_Q:C_Q + D_ATT], w_in[:, C_IQ:C_IQ + D_IQ], w_in[:, C_K:C_K + 2 * D_KV],
         w_in[:, C_IKW:split]], axis=1).T)
    return (norm_gain, w_pad, w_t, conv_w), (_mxu(w_up_conv), _mxu(w_up_attn), _mxu(w_out), final_gain)


def _prompt_layer(x, proj_w, merge_w, *, tm_proj, tm_merge, tq, cks):
    b, t, _ = x.shape
    prev = jnp.zeros((b, CONV_WIDTH - 1, D_CONV), x.dtype)
    yc, az, gg, cs, qt, iqt, kt, vt, kit, iwt, k_rows, ikw_rows, vt_blocks = _project(
        x, prev, *proj_w, tm=tm_proj, tokens_on_lanes=True)
    attn = _attn_prompt(qt, iqt, iwt, k_rows, ikw_rows, vt_blocks, tq=tq, cks=cks)
    flat = lambda a: a.reshape(b * t, a.shape[-1])
    y = _merge(flat(x), flat(yc), flat(attn), flat(az), flat(gg), *merge_w, tm=tm_merge)
    heads = lambda a: a.reshape(b, N_KV_HEADS, HEAD_DIM, t).transpose(0, 3, 1, 2)[None]
    return y.reshape(b, t, D_MODEL), heads(kt), heads(vt), kit.transpose(0, 2, 1)[None], cs[None]


def _sample_layer(x, prev, page_table, cache_k, cache_v, cache_kidx, proj_w, merge_w, *, tm_merge, kc):
    b, t, _ = x.shape
    n_phys = cache_k.shape[0]
    yc, az, gg, cs, q, iq, ikw, k, v = _project(x, prev, *proj_w, tm=t, tokens_on_lanes=False)
    cache_kt = cache_k.transpose(0, 2, 3, 1).reshape(n_phys, D_KV, PAGE_SIZE)
    cache_vt = cache_v.transpose(0, 2, 3, 1).reshape(n_phys, D_KV, PAGE_SIZE)
    cache_kit = cache_kidx.transpose(0, 2, 1)
    attn = _attn_sample(page_table, q, iq, ikw, k, v, cache_kt, cache_vt, cache_kit, kc=kc)
    flat = lambda a: a.reshape(b * t, a.shape[-1])
    y = _merge(flat(x), flat(yc), flat(attn), flat(az), flat(gg), *merge_w, tm=tm_merge)
    heads = lambda a: a.reshape(1, b, t, N_KV_HEADS, HEAD_DIM)
    return y.reshape(b, t, D_MODEL), heads(k), heads(v), ikw[None, :, :, :IDX_DIM], cs[None]


def kernel(x_prompt, x_sample, cache_k, cache_v, cache_kidx, state_conv, page_table, norm_gain, w_in,
           conv_w, w_up_conv, w_up_attn, w_out, final_gain):
    assert norm_gain.shape[0] == 1, "single-layer step"
    proj_w, merge_w = _prepare_weights(norm_gain[0], w_in[0], conv_w[0], w_up_conv[0], w_up_attn[0],
                                       w_out[0], final_gain)
    yp, kp, vp, ip, cp = _prompt_layer(x_prompt, proj_w, merge_w, tm_proj=256, tm_merge=512, tq=256, cks=512)
    ys, ks, vs, is_, cs = _sample_layer(x_sample, state_conv[0], page_table, cache_k[0], cache_v[0],
                                        cache_kidx[0], proj_w, merge_w, tm_merge=512, kc=1024)
    return (yp, ys, kp, vp, ip, cp, ks, vs, is_, cs)
```

```python
import functools

import jax
import jax.numpy as jnp
from jax import lax
from jax.experimental import pallas as pl
from jax.experimental.pallas import tpu as pltpu

D_MODEL = 1024
D_CONV = 512
CONV_WIDTH = 3
N_HEADS = 8
HEAD_DIM = 64
N_KV_HEADS = 2
GROUP = N_HEADS // N_KV_HEADS
D_ATT = N_HEADS * HEAD_DIM
D_KV = N_KV_HEADS * HEAD_DIM
N_IDX_HEADS = 8
IDX_DIM = 64
D_IQ = N_IDX_HEADS * IDX_DIM
TOPK_MAX = 256
PAGE_SIZE = 128
EPS = 1e-6

LANES = 128
SUBLANES = 8
VMEM_LIMIT = 48 * 1024 * 1024
MXU_DTYPE = jnp.bfloat16

C_CONV = 0
C_Q = 4 * D_CONV
C_K = C_Q + D_ATT
C_V = C_K + D_KV
C_AZ = C_V + D_KV
C_IQ = C_AZ + D_ATT
C_IKW = C_IQ + D_IQ
C_G = C_IKW + LANES
D_IN_RAW = C_IKW + IDX_DIM + N_IDX_HEADS + 2 * D_MODEL
D_IN_PAD = C_G + 2 * D_MODEL
R_Q = 0
R_IQ = R_Q + D_ATT
R_K = R_IQ + D_IQ
R_V = R_K + D_KV
R_IKW = R_V + D_KV
D_T = R_IKW + IDX_DIM + N_IDX_HEADS

MASKED_DIST = -1e33
MASKED_LOGIT = -1e30
LOG2E = 1.4426950408889634
N_LOG2E_PIECES = 4
POS_RADIX = 64
KEY_BLOCK = 2 * LANES
INT_MIN = -2 ** 31
KEY_NEG_INF = INT_MIN + 0x7FFFFF


def _dot(a, b):
    return jnp.dot(a, b, preferred_element_type=jnp.float32)


def _dot_nt(a, b):
    return lax.dot_general(a, b, (((1,), (1,)), ((), ())), preferred_element_type=jnp.float32)


def _mxu(x):
    return x.astype(MXU_DTYPE)


def _proj_kernel(x_ref, prev_ref, g_ref, w_ref, wt_ref, cw_ref, *refs, tm, tokens_on_lanes):
    yc_ref, az_ref, gg_ref, cs_ref = refs[:4]
    u_buf = refs[-1]
    t = pl.program_id(1)
    nt = pl.num_programs(1)
    x = x_ref[0]
    xn = x * lax.rsqrt(jnp.mean(x * x, axis=-1, keepdims=True) + EPS) * g_ref[...]
    xn = _mxu(xn)

    def proj(c0, width):
        return _dot(xn, w_ref[:, c0:c0 + width])

    def proj_t(r0, height):
        return _dot_nt(wt_ref[r0:r0 + height, :], xn)

    @pl.when(t == 0)
    def _():
        u_buf[0:SUBLANES, :] = jnp.zeros((SUBLANES, D_CONV), jnp.float32)
        u_buf[SUBLANES - 2:SUBLANES, :] = prev_ref[0]

    @pl.when(t > 0)
    def _():
        u_buf[0:SUBLANES, :] = u_buf[tm:tm + SUBLANES, :]

    u_buf[SUBLANES:SUBLANES + tm, :] = proj(C_CONV + D_CONV, D_CONV) * proj(C_CONV + 2 * D_CONV, D_CONV)
    conv = (cw_ref[0:1, :] * u_buf[SUBLANES - 2:SUBLANES - 2 + tm, :]
            + cw_ref[1:2, :] * u_buf[SUBLANES - 1:SUBLANES - 1 + tm, :]
            + cw_ref[2:3, :] * u_buf[SUBLANES:SUBLANES + tm, :])
    yc_ref[0] = (proj(C_CONV, D_CONV) * conv * jax.nn.silu(proj(C_CONV + 3 * D_CONV, D_CONV))).astype(yc_ref.dtype)

    @pl.when(t == nt - 1)
    def _():
        cs_ref[0] = u_buf[tm + SUBLANES - 2:tm + SUBLANES, :]

    az_ref[0] = proj(C_AZ, D_ATT)
    gg_ref[0] = proj(C_G, 2 * D_MODEL)
    if tokens_on_lanes:
        qt_ref, iqt_ref, kt_ref, vt_ref, kit_ref, iwt_ref, kr_ref, ikwr_ref, vtc_ref = refs[4:13]
        qt_ref[0] = _mxu(proj_t(R_Q, D_ATT) * (HEAD_DIM ** -0.5 * LOG2E))
        iqt_ref[0] = _mxu(proj_t(R_IQ, D_IQ) * (IDX_DIM ** -0.5))
        kt_ref[0] = proj_t(R_K, D_KV)
        vt = proj_t(R_V, D_KV)
        vt_ref[0] = vt
        for s in range(tm // KEY_BLOCK):
            vtc_ref[0, s] = _mxu(vt[:, s * KEY_BLOCK:(s + 1) * KEY_BLOCK])
        kit_ref[0] = proj_t(R_IKW, IDX_DIM)
        iwt_ref[0] = proj_t(R_IKW + IDX_DIM, N_IDX_HEADS)
        kr_ref[0, :, 0:D_KV] = _mxu(proj(C_K, D_KV))
        pos = t * tm + lax.broadcasted_iota(jnp.int32, (tm, LANES), 0)
        lane = lax.broadcasted_iota(jnp.int32, (tm, LANES), 1)
        feat = jnp.where(lane % 2 == 0, pos // POS_RADIX, pos % POS_RADIX)
        kr_ref[0, :, D_KV:D_KV + LANES] = _mxu(
            jnp.where(lane < 2 * N_LOG2E_PIECES, feat, 0).astype(jnp.float32))
        ikwr_ref[0] = _mxu(proj(C_IKW, LANES))
    else:
        q_ref, iq_ref, ikw_ref, k_ref, v_ref = refs[4:9]
        q_ref[0] = proj(C_Q, D_ATT) * (HEAD_DIM ** -0.5)
        iq_ref[0] = proj(C_IQ, D_IQ) * (IDX_DIM ** -0.5)
        ikw_ref[0] = proj(C_IKW, LANES)
        k_ref[0] = proj(C_K, D_KV)
        v_ref[0] = proj(C_V, D_KV)


def _project(x, prev, norm_gain, w_pad, w_t, conv_w, *, tm, tokens_on_lanes):
    b, t, _ = x.shape
    nt = t // tm
    row = lambda width: pl.BlockSpec((1, tm, width), lambda i, j: (i, j, 0))
    col = lambda height: pl.BlockSpec((1, height, tm), lambda i, j: (i, 0, j))
    full = lambda shape: pl.BlockSpec(shape, lambda i, j: (0,) * len(shape))
    state = pl.BlockSpec((1, CONV_WIDTH - 1, D_CONV), lambda i, j: (i, 0, 0))
    f32 = jnp.float32
    sds = jax.ShapeDtypeStruct
    out_shapes = [sds((b, t, D_CONV), MXU_DTYPE), sds((b, t, D_ATT), f32), sds((b, t, 2 * D_MODEL), f32),
                  sds((b, CONV_WIDTH - 1, D_CONV), f32)]
    out_specs = [row(D_CONV), row(D_ATT), row(2 * D_MODEL), state]
    if tokens_on_lanes:
        out_shapes += [sds((b, D_ATT, t), MXU_DTYPE), sds((b, D_IQ, t), MXU_DTYPE),
                       sds((b, D_KV, t), f32), sds((b, D_KV, t), f32), sds((b, IDX_DIM, t), f32),
                       sds((b, N_IDX_HEADS, t), f32),
                       sds((b, t, D_KV + LANES), MXU_DTYPE), sds((b, t, LANES), MXU_DTYPE),
                       sds((b, t // KEY_BLOCK, D_KV, KEY_BLOCK), MXU_DTYPE)]
        out_specs += [col(D_ATT), col(D_IQ), col(D_KV), col(D_KV), col(IDX_DIM), col(N_IDX_HEADS),
                      row(D_KV + LANES), row(LANES),
                      pl.BlockSpec((1, tm // KEY_BLOCK, D_KV, KEY_BLOCK), lambda i, j: (i, j, 0, 0))]
    else:
        out_shapes += [sds((b, t, D_ATT), f32), sds((b, t, D_IQ), f32), sds((b, t, LANES), f32),
                       sds((b, t, D_KV), f32), sds((b, t, D_KV), f32)]
        out_specs += [row(D_ATT), row(D_IQ), row(LANES), row(D_KV), row(D_KV)]
    return pl.pallas_call(
        functools.partial(_proj_kernel, tm=tm, tokens_on_lanes=tokens_on_lanes),
        grid=(b, nt),
        in_specs=[row(D_MODEL), state, full((1, D_MODEL)), full((D_MODEL, D_IN_PAD)),
                  full((D_T, D_MODEL)), full((CONV_WIDTH, D_CONV))],
        out_specs=out_specs,
        out_shape=out_shapes,
        scratch_shapes=[pltpu.VMEM((tm + SUBLANES, D_CONV), f32)],
        compiler_params=pltpu.CompilerParams(
            dimension_semantics=("arbitrary", "arbitrary"), vmem_limit_bytes=VMEM_LIMIT),
        name="proj",
    )(x, prev, norm_gain.reshape(1, D_MODEL), w_pad, w_t, conv_w)


def _merge_kernel(x_ref, yc_ref, at_ref, az_ref, gg_ref, wuc_ref, wua_ref, wo_ref, fg_ref, y_ref):
    ya = _mxu(at_ref[...] * jax.nn.silu(az_ref[...]))
    gc = gg_ref[:, 0:D_MODEL]
    ga = gg_ref[:, D_MODEL:2 * D_MODEL]
    m = (jax.nn.sigmoid(gc) * _dot(yc_ref[...], wuc_ref[...])
         + jax.nn.sigmoid(ga) * _dot(ya, wua_ref[...]))
    xo = x_ref[...] + _dot(_mxu(m), wo_ref[...])
    y_ref[...] = xo * lax.rsqrt(jnp.mean(xo * xo, axis=-1, keepdims=True) + EPS) * fg_ref[...]


def _merge(x, yc, attn, az, gg, wuc, wua, wo, final_gain, *, tm):
    n = x.shape[0]
    row = lambda width: pl.BlockSpec((tm, width), lambda i: (i, 0))
    full = lambda shape: pl.BlockSpec(shape, lambda i: (0,) * len(shape))
    return pl.pallas_call(
        _merge_kernel,
        grid=(n // tm,),
        in_specs=[row(D_MODEL), row(D_CONV), row(D_ATT), row(D_ATT), row(2 * D_MODEL),
                  full((D_CONV, D_MODEL)), full((D_ATT, D_MODEL)), full((D_MODEL, D_MODEL)),
                  full((1, D_MODEL))],
        out_specs=row(D_MODEL),
        out_shape=jax.ShapeDtypeStruct((n, D_MODEL), jnp.float32),
        compiler_params=pltpu.CompilerParams(
            dimension_semantics=("arbitrary",), vmem_limit_bytes=VMEM_LIMIT),
        name="merge",
    )(x, yc, attn, az, gg, wuc, wua, wo, final_gain.reshape(1, D_MODEL))


def _key_to_threshold(tu):
    key = tu ^ jnp.int32(INT_MIN)
    bits = key ^ ((key >> 31) & jnp.int32(0x7FFFFFFF))
    thr = lax.bitcast_convert_type(bits, jnp.float32)
    return jnp.where(key < jnp.int32(KEY_NEG_INF), -jnp.inf, thr)


def _lane_column(x, lane):
    ids = lax.broadcasted_iota(jnp.int32, x.shape, 1)
    return jnp.sum(jnp.where(ids == lane, x, 0.0), axis=1, keepdims=True)


def _swap_halves(x):
    return pltpu.roll(x, LANES // 2, axis=1)


def _head_slope(h):
    return 2.0 ** (-8.0 * (h + 1) / N_HEADS)


def _pack_head_pair(a0, a1, kv_head):
    lo = lax.broadcasted_iota(jnp.int32, a0.shape, 1) < LANES // 2
    if kv_head == 0:
        return jnp.where(lo, a0, _swap_halves(a1))
    return jnp.where(lo, _swap_halves(a0), a1)


N_COUNT_ACC = 4

def _alibi_rows(tq):
    pieces, rest = [], jnp.float32(LOG2E)
    for _ in range(N_LOG2E_PIECES):
        piece = rest.astype(MXU_DTYPE).astype(jnp.float32)
        pieces.append(piece)
        rest = rest - piece
    rows = jnp.zeros((N_HEADS, LANES), jnp.float32)
    for h in range(N_HEADS):
        for j, piece in enumerate(pieces):
            rows = rows.at[h, 2 * j].set(_head_slope(h) * POS_RADIX * piece)
            rows = rows.at[h, 2 * j + 1].set(_head_slope(h) * piece)
    return _mxu(jnp.broadcast_to(rows[:, :, None], (N_HEADS, LANES, tq)))


def _attn_prompt_kernel(qt_ref, iqt_ref, iwt_ref, kr_ref, ikwr_ref, vtc_ref, aw_ref, o_ref,
                        qw_s, iqw_s, sc_s, mask_s, m_s, l_s, a_s, acc_s, j_s, ot_s, s_s, p_s,
                        *, seq, tq, cks, k_sel):
    i = pl.program_id(1)
    n_kb = (i + 1) * (tq // KEY_BLOCK)
    n_cks = ((i + 1) * tq + cks - 1) // cks
    f32 = jnp.float32

    zeros = jnp.zeros((HEAD_DIM, tq), MXU_DTYPE)
    for h in range(N_HEADS):
        g = h // GROUP
        qw_s[h, g * HEAD_DIM:(g + 1) * HEAD_DIM, :] = qt_ref[0, h * HEAD_DIM:(h + 1) * HEAD_DIM, :]
        qw_s[h, (1 - g) * HEAD_DIM:(2 - g) * HEAD_DIM, :] = zeros
        qw_s[h, D_KV:D_KV + LANES, :] = aw_ref[h]
        iqw_s[h, 0:IDX_DIM, :] = iqt_ref[0, h * IDX_DIM:(h + 1) * IDX_DIM, :]
        iqw_s[h, IDX_DIM:2 * IDX_DIM, :] = zeros

    w_rows = iwt_ref[0] * (N_IDX_HEADS ** -0.5)
    q_pos_c = i * tq + lax.broadcasted_iota(jnp.int32, (cks, tq), 1)
    k_row_c = lax.broadcasted_iota(jnp.int32, (cks, tq), 0)

    def score_chunk(c, carry):
        kk = ikwr_ref[0, pl.ds(pl.multiple_of(c * cks, cks), cks), :]
        acc = jnp.zeros((cks, tq), f32)
        for h in range(N_IDX_HEADS):
            acc = acc + jnp.maximum(_dot(kk, iqw_s[h]), 0.0) * w_rows[h:h + 1, :]
        acc = jnp.where(c * cks + k_row_c <= q_pos_c, acc, -jnp.inf)
        for t in range(cks // KEY_BLOCK):
            sc_s[c * (cks // KEY_BLOCK) + t] = acc[t * KEY_BLOCK:(t + 1) * KEY_BLOCK]
        return carry
    lax.fori_loop(0, n_cks, score_chunk, 0)

    sub_id = lax.broadcasted_iota(jnp.int32, (SUBLANES, tq), 0)

    def count(pred):
        def body(j, parts):
            parts = list(parts)
            for r in range(KEY_BLOCK // SUBLANES):
                rows = sc_s[j, r * SUBLANES:(r + 1) * SUBLANES, :]
                hit = pred(rows, j * KEY_BLOCK + r * SUBLANES + sub_id)
                parts[r % N_COUNT_ACC] = parts[r % N_COUNT_ACC] + jnp.where(hit, 1.0, 0.0)
            return tuple(parts)
        parts = lax.fori_loop(0, n_kb, body, (jnp.zeros((SUBLANES, tq), f32),) * N_COUNT_ACC)
        return jnp.sum(functools.reduce(lambda a, b: a + b, parts), axis=0, keepdims=True)

    def rows8(row):
        return jnp.broadcast_to(row, (SUBLANES, tq))

    def bisect(it, carry):
        tu, cnt_acc = carry
        cand = tu | jnp.left_shift(jnp.int32(1), 31 - it)
        thr_c = rows8(_key_to_threshold(cand))
        cnt = count(lambda s, kp: s >= thr_c)
        ok = cnt >= float(k_sel)
        return jnp.where(ok, cand, tu), jnp.where(ok, cnt, cnt_acc)

    n_all = (n_kb * KEY_BLOCK).astype(f32)
    tu, cnt_ge = lax.fori_loop(
        0, 32, bisect, (jnp.zeros((1, tq), jnp.int32), jnp.zeros((1, tq), f32) + n_all))
    thr = _key_to_threshold(tu)

    j_s[...] = jnp.full((1, tq), seq, jnp.int32)

    @pl.when(jnp.max(cnt_ge) > float(k_sel))
    def _():
        thr8 = rows8(thr)
        need = float(k_sel) - count(lambda s, kp: s > thr8)
        nbits = (seq - 1).bit_length()
        def index_bit(it, p):
            cand = p + jnp.left_shift(jnp.int32(1), nbits - 1 - it)
            cand8 = rows8(cand)
            below = count(lambda s, kp: (s == thr8) & (kp < cand8))
            return jnp.where(below < need, cand, p)
        j_s[...] = lax.fori_loop(0, nbits, index_bit, jnp.zeros((1, tq), jnp.int32))

    m_s[...] = jnp.full(m_s.shape, -jnp.inf, f32)
    l_s[...] = jnp.zeros(l_s.shape, f32)
    acc_s[...] = jnp.zeros(acc_s.shape, f32)
    j_last = j_s[...]
    q_pos = i * tq + lax.broadcasted_iota(jnp.int32, (KEY_BLOCK, tq), 1)
    k_row = lax.broadcasted_iota(jnp.int32, (KEY_BLOCK, tq), 0)

    def attend_block(j, carry):
        k_pos = j * KEY_BLOCK + k_row
        s_idx = sc_s[j]
        sel = ((s_idx > thr) | ((s_idx == thr) & (k_pos <= j_last))) & (k_pos <= q_pos)
        mask_s[...] = jnp.where(sel, 0.0, MASKED_LOGIT)
        kk = kr_ref[0, pl.ds(pl.multiple_of(j * KEY_BLOCK, KEY_BLOCK), KEY_BLOCK), :]
        vt = vtc_ref[0, j]
        for h in range(N_HEADS):
            s_s[h] = _dot(kk, qw_s[h]) + mask_s[...]
        for h in range(N_HEADS):
            m_old = m_s[h:h + 1, :]
            m_new = jnp.maximum(m_old, jnp.max(s_s[h], axis=0, keepdims=True))
            alpha = jnp.exp2(m_old - m_new)
            p = jnp.exp2(s_s[h] - m_new)
            l_s[h:h + 1, :] = alpha * l_s[h:h + 1, :] + jnp.sum(p, axis=0, keepdims=True)
            p_s[h] = _mxu(p)
            m_s[h:h + 1, :] = m_new
            a_s[h:h + 1, :] = alpha
        for h in range(N_HEADS):
            g = h // GROUP
            acc_s[h] = a_s[h:h + 1, :] * acc_s[h] + _dot(vt[g * HEAD_DIM:(g + 1) * HEAD_DIM, :], p_s[h])
        return carry
    lax.fori_loop(0, n_kb, attend_block, 0)

    for h in range(N_HEADS):
        ot_s[h * HEAD_DIM:(h + 1) * HEAD_DIM, :] = acc_s[h] / l_s[h:h + 1, :]
    o_ref[0] = ot_s[...].T


def _attn_prompt(qt, iqt, iwt, k_rows, ikw_rows, vt_blocks, *, tq, cks):
    b, _, seq = qt.shape
    n_kb = seq // KEY_BLOCK
    k_sel = min(TOPK_MAX, seq // 4)
    colblk = lambda height: pl.BlockSpec((1, height, tq), lambda i, j: (i, 0, j))
    whole = lambda shape: pl.BlockSpec((1,) + shape, lambda i, j: (i,) + (0,) * len(shape))
    f32 = jnp.float32
    return pl.pallas_call(
        functools.partial(_attn_prompt_kernel, seq=seq, tq=tq, cks=cks, k_sel=k_sel),
        grid=(b, seq // tq),
        in_specs=[colblk(D_ATT), colblk(D_IQ), colblk(N_IDX_HEADS),
                  whole((seq, D_KV + LANES)), whole((seq, LANES)), whole((n_kb, D_KV, KEY_BLOCK)),
                  pl.BlockSpec((N_HEADS, LANES, tq), lambda i, j: (0, 0, 0))],
        out_specs=pl.BlockSpec((1, tq, D_ATT), lambda i, j: (i, j, 0)),
        out_shape=jax.ShapeDtypeStruct((b, seq, D_ATT), f32),
        scratch_shapes=[
            pltpu.VMEM((N_HEADS, D_KV + LANES, tq), MXU_DTYPE),
            pltpu.VMEM((N_IDX_HEADS, LANES, tq), MXU_DTYPE),
            pltpu.VMEM((n_kb, KEY_BLOCK, tq), f32),
            pltpu.VMEM((KEY_BLOCK, tq), f32),
            pltpu.VMEM((N_HEADS, tq), f32),
            pltpu.VMEM((N_HEADS, tq), f32),
            pltpu.VMEM((N_HEADS, tq), f32),
            pltpu.VMEM((N_HEADS, HEAD_DIM, tq), f32),
            pltpu.VMEM((1, tq), jnp.int32),
            pltpu.VMEM((D_ATT, tq), f32),
            pltpu.VMEM((N_HEADS, KEY_BLOCK, tq), f32),
            pltpu.VMEM((N_HEADS, KEY_BLOCK, tq), MXU_DTYPE),
        ],
        compiler_params=pltpu.CompilerParams(
            dimension_semantics=("arbitrary", "arbitrary"), vmem_limit_bytes=VMEM_LIMIT),
        name="attn_prompt",
    )(qt, iqt, iwt, k_rows, ikw_rows, vt_blocks, _alibi_rows(tq))


def _attn_sample_kernel(pt_ref, q_ref, iq_ref, ikw_ref, k_ref, v_ref, ck_hbm, cv_hbm, ci_hbm, o_ref,
                        kbuf, vbuf, ibuf, sc_s, s_s, sem,
                        *, n_pages, t_new, kc, k_sel):
    b = pl.program_id(0)
    nb = pl.num_programs(0)
    slot = b % 2
    past = n_pages * PAGE_SIZE
    n_rows = N_HEADS * t_new
    f32 = jnp.float32

    def page_copies(seq_id, seq_slot, p):
        phys = pt_ref[seq_id, p]
        cols = pl.ds(pl.multiple_of(p * PAGE_SIZE, PAGE_SIZE), PAGE_SIZE)
        return (pltpu.make_async_copy(ck_hbm.at[phys], kbuf.at[seq_slot, :, cols], sem.at[seq_slot, 0]),
                pltpu.make_async_copy(cv_hbm.at[phys], vbuf.at[seq_slot, :, cols], sem.at[seq_slot, 1]),
                pltpu.make_async_copy(ci_hbm.at[phys], ibuf.at[seq_slot, :, cols], sem.at[seq_slot, 2]))

    def start_pages(seq_id, seq_slot):
        def start_page(p, carry):
            for cp in page_copies(seq_id, seq_slot, p):
                cp.start()
            return carry
        lax.fori_loop(0, n_pages, start_page, 0)

    @pl.when(b == 0)
    def _():
        start_pages(0, 0)

    @pl.when(b + 1 < nb)
    def _():
        start_pages(b + 1, 1 - slot)

    lo = lax.broadcasted_iota(jnp.int32, (t_new, LANES), 1) < LANES // 2
    ikw = ikw_ref[0]
    iq_rows, q_rows, w_rows = [], [], []
    for h in range(N_HEADS):
        it = iq_ref[0, :, (h // 2) * LANES:(h // 2 + 1) * LANES]
        iq_rows.append((it if h % 2 == 0 else _swap_halves(it))[:, 0:IDX_DIM])
        qt = q_ref[0, :, (h // 2) * LANES:(h // 2 + 1) * LANES]
        g, par = h // GROUP, h % 2
        if g != par:
            qt = _swap_halves(qt)
        q_rows.append(jnp.where(lo, qt, 0.0) if g == 0 else jnp.where(lo, 0.0, qt))
        w_rows.append(_lane_column(ikw, IDX_DIM + h) * (N_IDX_HEADS ** -0.5))
    iq_all = _mxu(jnp.concatenate(iq_rows, axis=0))
    q_all = _mxu(jnp.concatenate(q_rows, axis=0))
    w_all = jnp.concatenate(w_rows, axis=0)
    slope = jnp.concatenate(
        [jnp.full((t_new, 1), _head_slope(h), f32) for h in range(N_HEADS)], axis=0)

    pad = jnp.zeros((LANES - t_new, LANES), f32)
    ik_new = _mxu(jnp.concatenate([ikw, pad], axis=0)[:, 0:IDX_DIM])
    k_new = _mxu(jnp.concatenate([k_ref[0], pad], axis=0))
    v_new = _mxu(jnp.concatenate([v_ref[0], pad], axis=0))

    def head_sum(r):
        acc = r[0:t_new]
        for h in range(1, N_HEADS):
            acc = acc + r[h * t_new:(h + 1) * t_new]
        return acc

    def wait_page(p, carry):
        for cp in page_copies(b, slot, p):
            cp.wait()
        return carry
    lax.fori_loop(0, n_pages, wait_page, 0)

    for c in range(past // kc):
        cols = slice(c * kc, (c + 1) * kc)
        sc_s[:, cols] = head_sum(jnp.maximum(_dot(iq_all, _mxu(ibuf[slot, :, cols])), 0.0) * w_all)
    t_id = lax.broadcasted_iota(jnp.int32, (t_new, LANES), 0)
    j_id = lax.broadcasted_iota(jnp.int32, (t_new, LANES), 1)
    s_new = head_sum(jnp.maximum(_dot_nt(iq_all, ik_new), 0.0) * w_all)
    sc_s[:, past:past + LANES] = jnp.where(j_id <= t_id, s_new, -jnp.inf)

    sc = sc_s[...]
    def count(mask):
        return jnp.sum(jnp.where(mask, 1.0, 0.0), axis=1, keepdims=True)

    def bisect(it, carry):
        tu, cnt_acc = carry
        shift = 30 - 2 * it
        for digit in (1, 2, 3):
            cand = tu | jnp.left_shift(jnp.int32(digit), shift)
            cnt = count(sc >= _key_to_threshold(cand))
            ok = cnt >= float(k_sel)
            best = jnp.where(ok, cand, tu if digit == 1 else best)
            cnt_acc = jnp.where(ok, cnt, cnt_acc)
        return best, cnt_acc

    tu, cnt_ge = lax.fori_loop(
        0, 16, bisect,
        (jnp.zeros((t_new, 1), jnp.int32), jnp.full((t_new, 1), float(past + LANES), f32)))
    thr = _key_to_threshold(tu)

    k_pos = lax.broadcasted_iota(jnp.int32, sc.shape, 1)
    q_pos = past + lax.broadcasted_iota(jnp.int32, sc.shape, 0)
    nbits = (past + LANES - 1).bit_length()

    def tie_limit():
        need = float(k_sel) - count(sc > thr)
        def index_bit(it, p):
            cand = p + jnp.left_shift(jnp.int32(1), nbits - 1 - it)
            below = count((sc == thr) & (k_pos < cand))
            return jnp.where(below < need, cand, p)
        return lax.fori_loop(0, nbits, index_bit, jnp.zeros((t_new, 1), jnp.int32))

    j_last = lax.cond(jnp.max(cnt_ge) > float(k_sel), tie_limit,
                      lambda: jnp.full((t_new, 1), past + LANES, jnp.int32))
    sel = ((sc > thr) | ((sc == thr) & (k_pos <= j_last))) & (k_pos <= q_pos)
    dist = jnp.where(sel, (k_pos - q_pos).astype(f32), MASKED_DIST)

    def logits(s, d):
        return s + slope * jnp.concatenate([d] * N_HEADS, axis=0)

    m = jnp.full((n_rows, 1), -jnp.inf, f32)
    for c in range(past // kc):
        cols = slice(c * kc, (c + 1) * kc)
        s = logits(_dot(q_all, _mxu(kbuf[slot, :, cols])), dist[:, cols])
        s_s[:, cols] = s
        m = jnp.maximum(m, jnp.max(s, axis=1, keepdims=True))
    s = logits(_dot_nt(q_all, k_new), dist[:, past:past + LANES])
    s_s[:, past:past + LANES] = s
    m = jnp.maximum(m, jnp.max(s, axis=1, keepdims=True))

    l = jnp.zeros((n_rows, 1), f32)
    o = jnp.zeros((n_rows, LANES), f32)
    for c in range(past // kc):
        cols = slice(c * kc, (c + 1) * kc)
        p = jnp.exp(s_s[:, cols] - m)
        l = l + jnp.sum(p, axis=1, keepdims=True)
        o = o + _dot_nt(_mxu(p), _mxu(vbuf[slot, :, cols]))
    p = jnp.exp(s_s[:, past:past + LANES] - m)
    l = l + jnp.sum(p, axis=1, keepdims=True)
    o = (o + _dot(_mxu(p), v_new)) / l

    for j in range(N_HEADS // 2):
        o_ref[0, :, j * LANES:(j + 1) * LANES] = _pack_head_pair(
            o[(2 * j) * t_new:(2 * j + 1) * t_new], o[(2 * j + 1) * t_new:(2 * j + 2) * t_new],
            (2 * j) // GROUP)


def _attn_sample(page_table, q, iq, ikw, k, v, cache_kt, cache_vt, cache_kit, *, kc):
    db, t_new, _ = q.shape
    n_pages = page_table.shape[1]
    past = n_pages * PAGE_SIZE
    k_sel = min(TOPK_MAX, (past + t_new) // 4)
    blk = lambda width: pl.BlockSpec((1, t_new, width), lambda i, pt: (i, 0, 0))
    hbm = pl.BlockSpec(memory_space=pl.ANY)
    f32 = jnp.float32
    return pl.pallas_call(
        functools.partial(_attn_sample_kernel, n_pages=n_pages, t_new=t_new, kc=kc, k_sel=k_sel),
        grid_spec=pltpu.PrefetchScalarGridSpec(
            num_scalar_prefetch=1,
            grid=(db,),
            in_specs=[blk(D_ATT), blk(D_IQ), blk(LANES), blk(D_KV), blk(D_KV), hbm, hbm, hbm],
            out_specs=blk(D_ATT),
            scratch_shapes=[
                pltpu.VMEM((2, D_KV, past), f32),
                pltpu.VMEM((2, D_KV, past), f32),
                pltpu.VMEM((2, IDX_DIM, past), f32),
                pltpu.VMEM((t_new, past + LANES), f32),
                pltpu.VMEM((N_HEADS * t_new, past + LANES), f32),
                pltpu.SemaphoreType.DMA((2, 3)),
            ]),
        out_shape=jax.ShapeDtypeStruct((db, t_new, D_ATT), f32),
        compiler_params=pltpu.CompilerParams(
            dimension_semantics=("arbitrary",), vmem_limit_bytes=VMEM_LIMIT),
        name="attn_sample",
    )(page_table, q, iq, ikw, k, v, cache_kt, cache_vt, cache_kit)


def _prepare_weights(norm_gain, w_in, conv_w, w_up_conv, w_up_attn, w_out, final_gain):
    split = C_IKW + IDX_DIM + N_IDX_HEADS
    pad = jnp.zeros((D_MODEL, D_IN_PAD - D_IN_RAW), w_in.dtype)
    w_pad = _mxu(jnp.concatenate([w_in[:, :split], pad, w_in[:, split:]], axis=1))
    w_t = _mxu(jnp.concatenate(
        [w_in[:, C_Q:C_Q + D_ATT], w_in[:, C_IQ:C_IQ + D_IQ], w_in[:, C_K:C_K + 2 * D_KV],
         w_in[:, C_IKW:split]], axis=1).T)
    return (norm_gain, w_pad, w_t, conv_w), (_mxu(w_up_conv), _mxu(w_up_attn), _mxu(w_out), final_gain)


def _prompt_layer(x, proj_w, merge_w, *, tm_proj, tm_merge, tq, cks):
    b, t, _ = x.shape
    prev = jnp.zeros((b, CONV_WIDTH - 1, D_CONV), x.dtype)
    yc, az, gg, cs, qt, iqt, kt, vt, kit, iwt, k_rows, ikw_rows, vt_blocks = _project(
        x, prev, *proj_w, tm=tm_proj, tokens_on_lanes=True)
    attn = _attn_prompt(qt, iqt, iwt, k_rows, ikw_rows, vt_blocks, tq=tq, cks=cks)
    flat = lambda a: a.reshape(b * t, a.shape[-1])
    y = _merge(flat(x), flat(yc), flat(attn), flat(az), flat(gg), *merge_w, tm=tm_merge)
    heads = lambda a: a.reshape(b, N_KV_HEADS, HEAD_DIM, t).transpose(0, 3, 1, 2)[None]
    return y.reshape(b, t, D_MODEL), heads(kt), heads(vt), kit.transpose(0, 2, 1)[None], cs[None]


def _sample_layer(x, prev, page_table, cache_k, cache_v, cache_kidx, proj_w, merge_w, *, tm_merge, kc):
    b, t, _ = x.shape
    n_phys = cache_k.shape[0]
    yc, az, gg, cs, q, iq, ikw, k, v = _project(x, prev, *proj_w, tm=t, tokens_on_lanes=False)
    cache_kt = cache_k.transpose(0, 2, 3, 1).reshape(n_phys, D_KV, PAGE_SIZE)
    cache_vt = cache_v.transpose(0, 2, 3, 1).reshape(n_phys, D_KV, PAGE_SIZE)
    cache_kit = cache_kidx.transpose(0, 2, 1)
    attn = _attn_sample(page_table, q, iq, ikw, k, v, cache_kt, cache_vt, cache_kit, kc=kc)
    flat = lambda a: a.reshape(b * t, a.shape[-1])
    y = _merge(flat(x), flat(yc), flat(attn), flat(az), flat(gg), *merge_w, tm=tm_merge)
    heads = lambda a: a.reshape(1, b, t, N_KV_HEADS, HEAD_DIM)
    return y.reshape(b, t, D_MODEL), heads(k), heads(v), ikw[None, :, :, :IDX_DIM], cs[None]


def kernel(x_prompt, x_sample, cache_k, cache_v, cache_kidx, state_conv, page_table, norm_gain, w_in,
           conv_w, w_up_conv, w_up_attn, w_out, final_gain):
    assert norm_gain.shape[0] == 1, "single-layer step"
    proj_w, merge_w = _prepare_weights(norm_gain[0], w_in[0], conv_w[0], w_up_conv[0], w_up_attn[0],
                                       w_out[0], final_gain)
    yp, kp, vp, ip, cp = _prompt_layer(x_prompt, proj_w, merge_w, tm_proj=256, tm_merge=512, tq=256, cks=512)
    ys, ks, vs, is_, cs = _sample_layer(x_sample, state_conv[0], page_table, cache_k[0], cache_v[0],
                                        cache_kidx[0], proj_w, merge_w, tm_merge=512, kc=1024)
    return (yp, ys, kp, vp, ip, cp, ks, vs, is_, cs)
```

```python
import functools

import jax
import jax.numpy as jnp
from jax import lax
from jax.experimental import pallas as pl
from jax.experimental.pallas import tpu as pltpu

D_MODEL = 1024
D_CONV = 512
CONV_WIDTH = 3
N_HEADS = 8
HEAD_DIM = 64
N_KV_HEADS = 2
GROUP = N_HEADS // N_KV_HEADS
D_ATT = N_HEADS * HEAD_DIM
D_KV = N_KV_HEADS * HEAD_DIM
N_IDX_HEADS = 8
IDX_DIM = 64
D_IQ = N_IDX_HEADS * IDX_DIM
TOPK_MAX = 256
PAGE_SIZE = 128
EPS = 1e-6

LANES = 128
SUBLANES = 8
VMEM_LIMIT = 48 * 1024 * 1024
MXU_DTYPE = jnp.bfloat16

C_CONV = 0
C_Q = 4 * D_CONV
C_K = C_Q + D_ATT
C_V = C_K + D_KV
C_AZ = C_V + D_KV
C_IQ = C_AZ + D_ATT
C_IKW = C_IQ + D_IQ
C_G = C_IKW + LANES
D_IN_RAW = C_IKW + IDX_DIM + N_IDX_HEADS + 2 * D_MODEL
D_IN_PAD = C_G + 2 * D_MODEL
R_Q = 0
R_IQ = R_Q + D_ATT
R_K = R_IQ + D_IQ
R_V = R_K + D_KV
R_IKW = R_V + D_KV
D_T = R_IKW + IDX_DIM + N_IDX_HEADS

MASKED_DIST = -1e33
MASKED_LOGIT = -1e30
LOG2E = 1.4426950408889634
N_LOG2E_PIECES = 4
POS_RADIX = 64
KEY_BLOCK = 2 * LANES
INT_MIN = -2 ** 31
KEY_NEG_INF = INT_MIN + 0x7FFFFF


def _dot(a, b):
    return jnp.dot(a, b, preferred_element_type=jnp.float32)


def _dot_nt(a, b):
    return lax.dot_general(a, b, (((1,), (1,)), ((), ())), preferred_element_type=jnp.float32)


def _mxu(x):
    return x.astype(MXU_DTYPE)


def _proj_kernel(x_ref, prev_ref, g_ref, w_ref, wt_ref, cw_ref, *refs, tm, tokens_on_lanes):
    yc_ref, az_ref, gg_ref, cs_ref = refs[:4]
    u_buf = refs[-1]
    t = pl.program_id(1)
    nt = pl.num_programs(1)
    x = x_ref[0]
    xn = x * lax.rsqrt(jnp.mean(x * x, axis=-1, keepdims=True) + EPS) * g_ref[...]
    xn = _mxu(xn)

    def proj(c0, width):
        return _dot(xn, w_ref[:, c0:c0 + width])

    def proj_t(r0, height):
        return _dot_nt(wt_ref[r0:r0 + height, :], xn)

    @pl.when(t == 0)
    def _():
        u_buf[0:SUBLANES, :] = jnp.zeros((SUBLANES, D_CONV), jnp.float32)
        u_buf[SUBLANES - 2:SUBLANES, :] = prev_ref[0]

    @pl.when(t > 0)
    def _():
        u_buf[0:SUBLANES, :] = u_buf[tm:tm + SUBLANES, :]

    u_buf[SUBLANES:SUBLANES + tm, :] = proj(C_CONV + D_CONV, D_CONV) * proj(C_CONV + 2 * D_CONV, D_CONV)
    conv = (cw_ref[0:1, :] * u_buf[SUBLANES - 2:SUBLANES - 2 + tm, :]
            + cw_ref[1:2, :] * u_buf[SUBLANES - 1:SUBLANES - 1 + tm, :]
            + cw_ref[2:3, :] * u_buf[SUBLANES:SUBLANES + tm, :])
    yc_ref[0] = (proj(C_CONV, D_CONV) * conv * jax.nn.silu(proj(C_CONV + 3 * D_CONV, D_CONV))).astype(yc_ref.dtype)

    @pl.when(t == nt - 1)
    def _():
        cs_ref[0] = u_buf[tm + SUBLANES - 2:tm + SUBLANES, :]

    az_ref[0] = proj(C_AZ, D_ATT)
    gg_ref[0] = proj(C_G, 2 * D_MODEL)
    if tokens_on_lanes:
        qt_ref, iqt_ref, kt_ref, vt_ref, kit_ref, iwt_ref, kr_ref, ikwr_ref, vtc_ref = refs[4:13]
        qt_ref[0] = _mxu(proj_t(R_Q, D_ATT) * (HEAD_DIM ** -0.5 * LOG2E))
        iqt_ref[0] = _mxu(proj_t(R_IQ, D_IQ) * (IDX_DIM ** -0.5))
        kt_ref[0] = proj_t(R_K, D_KV)
        vt = proj_t(R_V, D_KV)
        vt_ref[0] = vt
        for s in range(tm // KEY_BLOCK):
            vtc_ref[0, s] = _mxu(vt[:, s * KEY_BLOCK:(s + 1) * KEY_BLOCK])
        kit_ref[0] = proj_t(R_IKW, IDX_DIM)
        iwt_ref[0] = proj_t(R_IKW + IDX_DIM, N_IDX_HEADS)
        kr_ref[0, :, 0:D_KV] = _mxu(proj(C_K, D_KV))
        pos = t * tm + lax.broadcasted_iota(jnp.int32, (tm, LANES), 0)
        lane = lax.broadcasted_iota(jnp.int32, (tm, LANES), 1)
        feat = jnp.where(lane % 2 == 0, pos // POS_RADIX, pos % POS_RADIX)
        kr_ref[0, :, D_KV:D_KV + LANES] = _mxu(
            jnp.where(lane < 2 * N_LOG2E_PIECES, feat, 0).astype(jnp.float32))
        ikwr_ref[0] = _mxu(proj(C_IKW, LANES))
    else:
        q_ref, iq_ref, ikw_ref, k_ref, v_ref = refs[4:9]
        q_ref[0] = proj(C_Q, D_ATT) * (HEAD_DIM ** -0.5)
        iq_ref[0] = proj(C_IQ, D_IQ) * (IDX_DIM ** -0.5)
        ikw_ref[0] = proj(C_IKW, LANES)
        k_ref[0] = proj(C_K, D_KV)
        v_ref[0] = proj(C_V, D_KV)


def _project(x, prev, norm_gain, w_pad, w_t, conv_w, *, tm, tokens_on_lanes):
    b, t, _ = x.shape
    nt = t // tm
    row = lambda width: pl.BlockSpec((1, tm, width), lambda i, j: (i, j, 0))
    col = lambda height: pl.BlockSpec((1, height, tm), lambda i, j: (i, 0, j))
    full = lambda shape: pl.BlockSpec(shape, lambda i, j: (0,) * len(shape), pipeline_mode=pl.Buffered(1))
    state = pl.BlockSpec((1, CONV_WIDTH - 1, D_CONV), lambda i, j: (i, 0, 0))
    f32 = jnp.float32
    sds = jax.ShapeDtypeStruct
    out_shapes = [sds((b, t, D_CONV), MXU_DTYPE), sds((b, t, D_ATT), f32), sds((b, t, 2 * D_MODEL), f32),
                  sds((b, CONV_WIDTH - 1, D_CONV), f32)]
    out_specs = [row(D_CONV), row(D_ATT), row(2 * D_MODEL), state]
    if tokens_on_lanes:
        out_shapes += [sds((b, D_ATT, t), MXU_DTYPE), sds((b, D_IQ, t), MXU_DTYPE),
                       sds((b, D_KV, t), f32), sds((b, D_KV, t), f32), sds((b, IDX_DIM, t), f32),
                       sds((b, N_IDX_HEADS, t), f32),
                       sds((b, t, D_KV + LANES), MXU_DTYPE), sds((b, t, LANES), MXU_DTYPE),
                       sds((b, t // KEY_BLOCK, D_KV, KEY_BLOCK), MXU_DTYPE)]
        out_specs += [col(D_ATT), col(D_IQ), col(D_KV), col(D_KV), col(IDX_DIM), col(N_IDX_HEADS),
                      row(D_KV + LANES), row(LANES),
                      pl.BlockSpec((1, tm // KEY_BLOCK, D_KV, KEY_BLOCK), lambda i, j: (i, j, 0, 0))]
    else:
        out_shapes += [sds((b, t, D_ATT), f32), sds((b, t, D_IQ), f32), sds((b, t, LANES), f32),
                       sds((b, t, D_KV), f32), sds((b, t, D_KV), f32)]
        out_specs += [row(D_ATT), row(D_IQ), row(LANES), row(D_KV), row(D_KV)]
    return pl.pallas_call(
        functools.partial(_proj_kernel, tm=tm, tokens_on_lanes=tokens_on_lanes),
        grid=(b, nt),
        in_specs=[row(D_MODEL), state, full((1, D_MODEL)), full((D_MODEL, D_IN_PAD)),
                  full((D_T, D_MODEL)), full((CONV_WIDTH, D_CONV))],
        out_specs=out_specs,
        out_shape=out_shapes,
        scratch_shapes=[pltpu.VMEM((tm + SUBLANES, D_CONV), f32)],
        compiler_params=pltpu.CompilerParams(
            dimension_semantics=("arbitrary", "arbitrary"), vmem_limit_bytes=VMEM_LIMIT),
        name="proj",
    )(x, prev, norm_gain.reshape(1, D_MODEL), w_pad, w_t, conv_w)


def _merge_kernel(x_ref, yc_ref, at_ref, az_ref, gg_ref, wuc_ref, wua_ref, wo_ref, fg_ref, y_ref):
    ya = _mxu(at_ref[...] * jax.nn.silu(az_ref[...]))
    gc = gg_ref[:, 0:D_MODEL]
    ga = gg_ref[:, D_MODEL:2 * D_MODEL]
    m = (jax.nn.sigmoid(gc) * _dot(yc_ref[...], wuc_ref[...])
         + jax.nn.sigmoid(ga) * _dot(ya, wua_ref[...]))
    xo = x_ref[...] + _dot(_mxu(m), wo_ref[...])
    y_ref[...] = xo * lax.rsqrt(jnp.mean(xo * xo, axis=-1, keepdims=True) + EPS) * fg_ref[...]


def _merge(x, yc, attn, az, gg, wuc, wua, wo, final_gain, *, tm):
    n = x.shape[0]
    row = lambda width: pl.BlockSpec((tm, width), lambda i: (i, 0))
    full = lambda shape: pl.BlockSpec(shape, lambda i: (0,) * len(shape), pipeline_mode=pl.Buffered(1))
    return pl.pallas_call(
        _merge_kernel,
        grid=(n // tm,),
        in_specs=[row(D_MODEL), row(D_CONV), row(D_ATT), row(D_ATT), row(2 * D_MODEL),
                  full((D_CONV, D_MODEL)), full((D_ATT, D_MODEL)), full((D_MODEL, D_MODEL)),
                  full((1, D_MODEL))],
        out_specs=row(D_MODEL),
        out_shape=jax.ShapeDtypeStruct((n, D_MODEL), jnp.float32),
        compiler_params=pltpu.CompilerParams(
            dimension_semantics=("arbitrary",), vmem_limit_bytes=VMEM_LIMIT),
        name="merge",
    )(x, yc, attn, az, gg, wuc, wua, wo, final_gain.reshape(1, D_MODEL))


def _key_to_threshold(tu):
    key = tu ^ jnp.int32(INT_MIN)
    bits = key ^ ((key >> 31) & jnp.int32(0x7FFFFFFF))
    thr = lax.bitcast_convert_type(bits, jnp.float32)
    return jnp.where(key < jnp.int32(KEY_NEG_INF), -jnp.inf, thr)


PACKED_DTYPE = jnp.bfloat16
PACKED_ROWS = 2 * SUBLANES


def _high_half(x):
    bits = lax.bitcast_convert_type(x, jnp.int32) & jnp.int32(-65536)
    return lax.bitcast_convert_type(bits, jnp.float32).astype(PACKED_DTYPE)


def _lane_column(x, lane):
    ids = lax.broadcasted_iota(jnp.int32, x.shape, 1)
    return jnp.sum(jnp.where(ids == lane, x, 0.0), axis=1, keepdims=True)


def _swap_halves(x):
    return pltpu.roll(x, LANES // 2, axis=1)


def _head_slope(h):
    return 2.0 ** (-8.0 * (h + 1) / N_HEADS)


def _pack_head_pair(a0, a1, kv_head):
    lo = lax.broadcasted_iota(jnp.int32, a0.shape, 1) < LANES // 2
    if kv_head == 0:
        return jnp.where(lo, a0, _swap_halves(a1))
    return jnp.where(lo, _swap_halves(a0), a1)


N_COUNT_ACC = 4

def _alibi_rows(tq):
    pieces, rest = [], jnp.float32(LOG2E)
    for _ in range(N_LOG2E_PIECES):
        piece = rest.astype(MXU_DTYPE).astype(jnp.float32)
        pieces.append(piece)
        rest = rest - piece
    rows = jnp.zeros((N_HEADS, LANES), jnp.float32)
    for h in range(N_HEADS):
        for j, piece in enumerate(pieces):
            rows = rows.at[h, 2 * j].set(_head_slope(h) * POS_RADIX * piece)
            rows = rows.at[h, 2 * j + 1].set(_head_slope(h) * piece)
    return _mxu(jnp.broadcast_to(rows[:, :, None], (N_HEADS, LANES, tq)))


def _attn_prompt_kernel(qt_ref, iqt_ref, iwt_ref, kr_ref, ikwr_ref, vtc_ref, aw_ref, o_ref,
                        qw_s, iqw_s, sc_s, sh_s, mask_s, m_s, l_s, a_s, acc_s, j_s, ot_s, s_s, p_s,
                        *, seq, tq, cks, k_sel):
    i = pl.program_id(1)
    n_kb = (i + 1) * (tq // KEY_BLOCK)
    n_cks = ((i + 1) * tq + cks - 1) // cks
    f32 = jnp.float32

    zeros = jnp.zeros((HEAD_DIM, tq), MXU_DTYPE)
    for h in range(N_HEADS):
        g = h // GROUP
        qw_s[h, g * HEAD_DIM:(g + 1) * HEAD_DIM, :] = qt_ref[0, h * HEAD_DIM:(h + 1) * HEAD_DIM, :]
        qw_s[h, (1 - g) * HEAD_DIM:(2 - g) * HEAD_DIM, :] = zeros
        qw_s[h, D_KV:D_KV + LANES, :] = aw_ref[h]
        iqw_s[h, 0:IDX_DIM, :] = iqt_ref[0, h * IDX_DIM:(h + 1) * IDX_DIM, :]
        iqw_s[h, IDX_DIM:2 * IDX_DIM, :] = zeros

    w_rows = iwt_ref[0] * (N_IDX_HEADS ** -0.5)
    q_pos_c = i * tq + lax.broadcasted_iota(jnp.int32, (cks, tq), 1)
    k_row_c = lax.broadcasted_iota(jnp.int32, (cks, tq), 0)

    def score_chunk(c, carry):
        kk = ikwr_ref[0, pl.ds(pl.multiple_of(c * cks, cks), cks), :]
        acc = jnp.zeros((cks, tq), f32)
        for h in range(N_IDX_HEADS):
            acc = acc + jnp.maximum(_dot(kk, iqw_s[h]), 0.0) * w_rows[h:h + 1, :]
        acc = jnp.where(c * cks + k_row_c <= q_pos_c, acc, -jnp.inf)
        for t in range(cks // KEY_BLOCK):
            blk = acc[t * KEY_BLOCK:(t + 1) * KEY_BLOCK]
            sc_s[c * (cks // KEY_BLOCK) + t] = blk
            sh_s[c * (cks // KEY_BLOCK) + t] = _high_half(blk)
        return carry
    lax.fori_loop(0, n_cks, score_chunk, 0)

    sub_id = lax.broadcasted_iota(jnp.int32, (SUBLANES, tq), 0)

    def count(pred):
        def body(j, parts):
            parts = list(parts)
            for r in range(KEY_BLOCK // SUBLANES):
                rows = sc_s[j, r * SUBLANES:(r + 1) * SUBLANES, :]
                hit = pred(rows, j * KEY_BLOCK + r * SUBLANES + sub_id)
                parts[r % N_COUNT_ACC] = parts[r % N_COUNT_ACC] + jnp.where(hit, 1.0, 0.0)
            return tuple(parts)
        parts = lax.fori_loop(0, n_kb, body, (jnp.zeros((SUBLANES, tq), f32),) * N_COUNT_ACC)
        return jnp.sum(functools.reduce(lambda a, b: a + b, parts), axis=0, keepdims=True)

    def rows8(row):
        return jnp.broadcast_to(row, (SUBLANES, tq))

    def count_high(thr_high):
        one = jnp.ones((PACKED_ROWS, tq), PACKED_DTYPE)
        zero = jnp.zeros((PACKED_ROWS, tq), PACKED_DTYPE)
        def body(j, parts):
            parts = list(parts)
            for r in range(KEY_BLOCK // PACKED_ROWS):
                rows = sh_s[j, r * PACKED_ROWS:(r + 1) * PACKED_ROWS, :]
                parts[r % N_COUNT_ACC] = parts[r % N_COUNT_ACC] + jnp.where(rows >= thr_high, one, zero)
            return tuple(parts)
        parts = lax.fori_loop(0, n_kb, body, (zero,) * N_COUNT_ACC)
        total = functools.reduce(lambda a, b: a + b, [p.astype(f32) for p in parts])
        return jnp.sum(total, axis=0, keepdims=True)

    def bisect(it, carry, count_ge):
        tu, cnt_acc = carry
        cand = tu | jnp.left_shift(jnp.int32(1), 31 - it)
        cnt = count_ge(_key_to_threshold(cand))
        ok = cnt >= float(k_sel)
        return jnp.where(ok, cand, tu), jnp.where(ok, cnt, cnt_acc)

    def count_ge_high(thr_c):
        return count_high(jnp.broadcast_to(_high_half(thr_c), (PACKED_ROWS, tq)))

    def count_ge_full(thr_c):
        thr8 = rows8(thr_c)
        return count(lambda s, kp: s >= thr8)

    n_all = (n_kb * KEY_BLOCK).astype(f32)
    carry = (jnp.zeros((1, tq), jnp.int32), jnp.zeros((1, tq), f32) + n_all)
    carry = lax.fori_loop(0, 16, functools.partial(bisect, count_ge=count_ge_high), carry)
    tu, cnt_ge = lax.fori_loop(16, 32, functools.partial(bisect, count_ge=count_ge_full), carry)
    thr = _key_to_threshold(tu)

    j_s[...] = jnp.full((1, tq), seq, jnp.int32)

    @pl.when(jnp.max(cnt_ge) > float(k_sel))
    def _():
        thr8 = rows8(thr)
        need = float(k_sel) - count(lambda s, kp: s > thr8)
        nbits = (seq - 1).bit_length()
        def index_bit(it, p):
            cand = p + jnp.left_shift(jnp.int32(1), nbits - 1 - it)
            cand8 = rows8(cand)
            below = count(lambda s, kp: (s == thr8) & (kp < cand8))
            return jnp.where(below < need, cand, p)
        j_s[...] = lax.fori_loop(0, nbits, index_bit, jnp.zeros((1, tq), jnp.int32))

    m_s[...] = jnp.full(m_s.shape, -jnp.inf, f32)
    l_s[...] = jnp.zeros(l_s.shape, f32)
    acc_s[...] = jnp.zeros(acc_s.shape, f32)
    j_last = j_s[...]
    q_pos = i * tq + lax.broadcasted_iota(jnp.int32, (KEY_BLOCK, tq), 1)
    k_row = lax.broadcasted_iota(jnp.int32, (KEY_BLOCK, tq), 0)

    def attend_block(j, carry):
        k_pos = j * KEY_BLOCK + k_row
        s_idx = sc_s[j]
        sel = ((s_idx > thr) | ((s_idx == thr) & (k_pos <= j_last))) & (k_pos <= q_pos)
        mask_s[...] = jnp.where(sel, 0.0, MASKED_LOGIT)
        kk = kr_ref[0, pl.ds(pl.multiple_of(j * KEY_BLOCK, KEY_BLOCK), KEY_BLOCK), :]
        vt = vtc_ref[0, j]
        for h in range(N_HEADS):
            s_s[h] = _dot(kk, qw_s[h]) + mask_s[...]
        for h in range(N_HEADS):
            m_old = m_s[h:h + 1, :]
            m_new = jnp.maximum(m_old, jnp.max(s_s[h], axis=0, keepdims=True))
            alpha = jnp.exp2(m_old - m_new)
            p = jnp.exp2(s_s[h] - m_new)
            l_s[h:h + 1, :] = alpha * l_s[h:h + 1, :] + jnp.sum(p, axis=0, keepdims=True)
            p_s[h] = _mxu(p)
            m_s[h:h + 1, :] = m_new
            a_s[h:h + 1, :] = alpha
        for h in range(N_HEADS):
            g = h // GROUP
            acc_s[h] = a_s[h:h + 1, :] * acc_s[h] + _dot(vt[g * HEAD_DIM:(g + 1) * HEAD_DIM, :], p_s[h])
        return carry
    lax.fori_loop(0, n_kb, attend_block, 0)

    for h in range(N_HEADS):
        ot_s[h * HEAD_DIM:(h + 1) * HEAD_DIM, :] = acc_s[h] / l_s[h:h + 1, :]
    o_ref[0] = ot_s[...].T


def _attn_prompt(qt, iqt, iwt, k_rows, ikw_rows, vt_blocks, *, tq, cks):
    b, _, seq = qt.shape
    n_kb = seq // KEY_BLOCK
    k_sel = min(TOPK_MAX, seq // 4)
    colblk = lambda height: pl.BlockSpec((1, height, tq), lambda i, j: (i, 0, j))
    whole = lambda shape: pl.BlockSpec((1,) + shape, lambda i, j: (i,) + (0,) * len(shape))
    f32 = jnp.float32
    return pl.pallas_call(
        functools.partial(_attn_prompt_kernel, seq=seq, tq=tq, cks=cks, k_sel=k_sel),
        grid=(b, seq // tq),
        in_specs=[colblk(D_ATT), colblk(D_IQ), colblk(N_IDX_HEADS),
                  whole((seq, D_KV + LANES)), whole((seq, LANES)), whole((n_kb, D_KV, KEY_BLOCK)),
                  pl.BlockSpec((N_HEADS, LANES, tq), lambda i, j: (0, 0, 0))],
        out_specs=pl.BlockSpec((1, tq, D_ATT), lambda i, j: (i, j, 0)),
        out_shape=jax.ShapeDtypeStruct((b, seq, D_ATT), f32),
        scratch_shapes=[
            pltpu.VMEM((N_HEADS, D_KV + LANES, tq), MXU_DTYPE),
            pltpu.VMEM((N_IDX_HEADS, LANES, tq), MXU_DTYPE),
            pltpu.VMEM((n_kb, KEY_BLOCK, tq), f32),
            pltpu.VMEM((n_kb, KEY_BLOCK, tq), PACKED_DTYPE),
            pltpu.VMEM((KEY_BLOCK, tq), f32),
            pltpu.VMEM((N_HEADS, tq), f32),
            pltpu.VMEM((N_HEADS, tq), f32),
            pltpu.VMEM((N_HEADS, tq), f32),
            pltpu.VMEM((N_HEADS, HEAD_DIM, tq), f32),
            pltpu.VMEM((1, tq), jnp.int32),
            pltpu.VMEM((D_ATT, tq), f32),
            pltpu.VMEM((N_HEADS, KEY_BLOCK, tq), f32),
            pltpu.VMEM((N_HEADS, KEY_BLOCK, tq), MXU_DTYPE),
        ],
        compiler_params=pltpu.CompilerParams(
            dimension_semantics=("arbitrary", "arbitrary"), vmem_limit_bytes=VMEM_LIMIT),
        name="attn_prompt",
    )(qt, iqt, iwt, k_rows, ikw_rows, vt_blocks, _alibi_rows(tq))


def _attn_sample_kernel(pt_ref, q_ref, iq_ref, ikw_ref, k_ref, v_ref, ck_hbm, cv_hbm, ci_hbm, o_ref,
                        kbuf, vbuf, ibuf, sc_s, s_s, sem,
                        *, n_pages, t_new, kc, k_sel):
    b = pl.program_id(0)
    nb = pl.num_programs(0)
    slot = b % 2
    past = n_pages * PAGE_SIZE
    n_rows = N_HEADS * t_new
    f32 = jnp.float32

    def page_copies(seq_id, seq_slot, p):
        phys = pt_ref[seq_id, p]
        cols = pl.ds(pl.multiple_of(p * PAGE_SIZE, PAGE_SIZE), PAGE_SIZE)
        return (pltpu.make_async_copy(ck_hbm.at[phys], kbuf.at[seq_slot, :, cols], sem.at[seq_slot, 0]),
                pltpu.make_async_copy(cv_hbm.at[phys], vbuf.at[seq_slot, :, cols], sem.at[seq_slot, 1]),
                pltpu.make_async_copy(ci_hbm.at[phys], ibuf.at[seq_slot, :, cols], sem.at[seq_slot, 2]))

    def start_pages(seq_id, seq_slot):
        def start_page(p, carry):
            for cp in page_copies(seq_id, seq_slot, p):
                cp.start()
            return carry
        lax.fori_loop(0, n_pages, start_page, 0)

    @pl.when(b == 0)
    def _():
        start_pages(0, 0)

    @pl.when(b + 1 < nb)
    def _():
        start_pages(b + 1, 1 - slot)

    lo = lax.broadcasted_iota(jnp.int32, (t_new, LANES), 1) < LANES // 2
    ikw = ikw_ref[0]
    iq_rows, q_rows, w_rows = [], [], []
    for h in range(N_HEADS):
        it = iq_ref[0, :, (h // 2) * LANES:(h // 2 + 1) * LANES]
        iq_rows.append((it if h % 2 == 0 else _swap_halves(it))[:, 0:IDX_DIM])
        qt = q_ref[0, :, (h // 2) * LANES:(h // 2 + 1) * LANES]
        g, par = h // GROUP, h % 2
        if g != par:
            qt = _swap_halves(qt)
        q_rows.append(jnp.where(lo, qt, 0.0) if g == 0 else jnp.where(lo, 0.0, qt))
        w_rows.append(_lane_column(ikw, IDX_DIM + h) * (N_IDX_HEADS ** -0.5))
    iq_all = _mxu(jnp.concatenate(iq_rows, axis=0))
    q_all = _mxu(jnp.concatenate(q_rows, axis=0))
    w_all = jnp.concatenate(w_rows, axis=0)
    slope = jnp.concatenate(
        [jnp.full((t_new, 1), _head_slope(h), f32) for h in range(N_HEADS)], axis=0)

    pad = jnp.zeros((LANES - t_new, LANES), f32)
    ik_new = _mxu(jnp.concatenate([ikw, pad], axis=0)[:, 0:IDX_DIM])
    k_new = _mxu(jnp.concatenate([k_ref[0], pad], axis=0))
    v_new = _mxu(jnp.concatenate([v_ref[0], pad], axis=0))

    def head_sum(r):
        acc = r[0:t_new]
        for h in range(1, N_HEADS):
            acc = acc + r[h * t_new:(h + 1) * t_new]
        return acc

    def wait_page(p, carry):
        for cp in page_copies(b, slot, p):
            cp.wait()
        return carry
    lax.fori_loop(0, n_pages, wait_page, 0)

    for c in range(past // kc):
        cols = slice(c * kc, (c + 1) * kc)
        sc_s[:, cols] = head_sum(jnp.maximum(_dot(iq_all, _mxu(ibuf[slot, :, cols])), 0.0) * w_all)
    t_id = lax.broadcasted_iota(jnp.int32, (t_new, LANES), 0)
    j_id = lax.broadcasted_iota(jnp.int32, (t_new, LANES), 1)
    s_new = head_sum(jnp.maximum(_dot_nt(iq_all, ik_new), 0.0) * w_all)
    sc_s[:, past:past + LANES] = jnp.where(j_id <= t_id, s_new, -jnp.inf)

    sc = sc_s[...]
    def count(mask):
        return jnp.sum(jnp.where(mask, 1.0, 0.0), axis=1, keepdims=True)

    def bisect(it, carry):
        tu, cnt_acc = carry
        shift = 30 - 2 * it
        for digit in (1, 2, 3):
            cand = tu | jnp.left_shift(jnp.int32(digit), shift)
            cnt = count(sc >= _key_to_threshold(cand))
            ok = cnt >= float(k_sel)
            best = jnp.where(ok, cand, tu if digit == 1 else best)
            cnt_acc = jnp.where(ok, cnt, cnt_acc)
        return best, cnt_acc

    tu, cnt_ge = lax.fori_loop(
        0, 16, bisect,
        (jnp.zeros((t_new, 1), jnp.int32), jnp.full((t_new, 1), float(past + LANES), f32)))
    thr = _key_to_threshold(tu)

    k_pos = lax.broadcasted_iota(jnp.int32, sc.shape, 1)
    q_pos = past + lax.broadcasted_iota(jnp.int32, sc.shape, 0)
    nbits = (past + LANES - 1).bit_length()

    def tie_limit():
        need = float(k_sel) - count(sc > thr)
        def index_bit(it, p):
            cand = p + jnp.left_shift(jnp.int32(1), nbits - 1 - it)
            below = count((sc == thr) & (k_pos < cand))
            return jnp.where(below < need, cand, p)
        return lax.fori_loop(0, nbits, index_bit, jnp.zeros((t_new, 1), jnp.int32))

    j_last = lax.cond(jnp.max(cnt_ge) > float(k_sel), tie_limit,
                      lambda: jnp.full((t_new, 1), past + LANES, jnp.int32))
    sel = ((sc > thr) | ((sc == thr) & (k_pos <= j_last))) & (k_pos <= q_pos)
    dist = jnp.where(sel, (k_pos - q_pos).astype(f32), MASKED_DIST)

    def logits(s, d):
        return s + slope * jnp.concatenate([d] * N_HEADS, axis=0)

    m = jnp.full((n_rows, 1), -jnp.inf, f32)
    for c in range(past // kc):
        cols = slice(c * kc, (c + 1) * kc)
        s = logits(_dot(q_all, _mxu(kbuf[slot, :, cols])), dist[:, cols])
        s_s[:, cols] = s
        m = jnp.maximum(m, jnp.max(s, axis=1, keepdims=True))
    s = logits(_dot_nt(q_all, k_new), dist[:, past:past + LANES])
    s_s[:, past:past + LANES] = s
    m = jnp.maximum(m, jnp.max(s, axis=1, keepdims=True))

    l = jnp.zeros((n_rows, 1), f32)
    o = jnp.zeros((n_rows, LANES), f32)
    for c in range(past // kc):
        cols = slice(c * kc, (c + 1) * kc)
        p = jnp.exp(s_s[:, cols] - m)
        l = l + jnp.sum(p, axis=1, keepdims=True)
        o = o + _dot_nt(_mxu(p), _mxu(vbuf[slot, :, cols]))
    p = jnp.exp(s_s[:, past:past + LANES] - m)
    l = l + jnp.sum(p, axis=1, keepdims=True)
    o = (o + _dot(_mxu(p), v_new)) / l

    for j in range(N_HEADS // 2):
        o_ref[0, :, j * LANES:(j + 1) * LANES] = _pack_head_pair(
            o[(2 * j) * t_new:(2 * j + 1) * t_new], o[(2 * j + 1) * t_new:(2 * j + 2) * t_new],
            (2 * j) // GROUP)


def _attn_sample(page_table, q, iq, ikw, k, v, cache_kt, cache_vt, cache_kit, *, kc):
    db, t_new, _ = q.shape
    n_pages = page_table.shape[1]
    past = n_pages * PAGE_SIZE
    k_sel = min(TOPK_MAX, (past + t_new) // 4)
    blk = lambda width: pl.BlockSpec((1, t_new, width), lambda i, pt: (i, 0, 0))
    hbm = pl.BlockSpec(memory_space=pl.ANY)
    f32 = jnp.float32
    return pl.pallas_call(
        functools.partial(_attn_sample_kernel, n_pages=n_pages, t_new=t_new, kc=kc, k_sel=k_sel),
        grid_spec=pltpu.PrefetchScalarGridSpec(
            num_scalar_prefetch=1,
            grid=(db,),
            in_specs=[blk(D_ATT), blk(D_IQ), blk(LANES), blk(D_KV), blk(D_KV), hbm, hbm, hbm],
            out_specs=blk(D_ATT),
            scratch_shapes=[
                pltpu.VMEM((2, D_KV, past), f32),
                pltpu.VMEM((2, D_KV, past), f32),
                pltpu.VMEM((2, IDX_DIM, past), f32),
                pltpu.VMEM((t_new, past + LANES), f32),
                pltpu.VMEM((N_HEADS * t_new, past + LANES), f32),
                pltpu.SemaphoreType.DMA((2, 3)),
            ]),
        out_shape=jax.ShapeDtypeStruct((db, t_new, D_ATT), f32),
        compiler_params=pltpu.CompilerParams(
            dimension_semantics=("arbitrary",), vmem_limit_bytes=VMEM_LIMIT),
        name="attn_sample",
    )(page_table, q, iq, ikw, k, v, cache_kt, cache_vt, cache_kit)


def _prepare_weights(norm_gain, w_in, conv_w, w_up_conv, w_up_attn, w_out, final_gain):
    split = C_IKW + IDX_DIM + N_IDX_HEADS
    pad = jnp.zeros((D_MODEL, D_IN_PAD - D_IN_RAW), w_in.dtype)
    w_pad = _mxu(jnp.concatenate([w_in[:, :split], pad, w_in[:, split:]], axis=1))
    w_t = _mxu(jnp.concatenate(
        [w_in[:, C_Q:C_Q + D_ATT], w_in[:, C_IQ:C_IQ + D_IQ], w_in[:, C_K:C_K + 2 * D_KV],
         w_in[:, C_IKW:split]], axis=1).T)
    return (norm_gain, w_pad, w_t, conv_w), (_mxu(w_up_conv), _mxu(w_up_attn), _mxu(w_out), final_gain)


def _prompt_layer(x, proj_w, merge_w, *, tm_proj, tm_merge, tq, cks):
    b, t, _ = x.shape
    prev = jnp.zeros((b, CONV_WIDTH - 1, D_CONV), x.dtype)
    yc, az, gg, cs, qt, iqt, kt, vt, kit, iwt, k_rows, ikw_rows, vt_blocks = _project(
        x, prev, *proj_w, tm=tm_proj, tokens_on_lanes=True)
    attn = _attn_prompt(qt, iqt, iwt, k_rows, ikw_rows, vt_blocks, tq=tq, cks=cks)
    flat = lambda a: a.reshape(b * t, a.shape[-1])
    y = _merge(flat(x), flat(yc), flat(attn), flat(az), flat(gg), *merge_w, tm=tm_merge)
    heads = lambda a: a.reshape(b, N_KV_HEADS, HEAD_DIM, t).transpose(0, 3, 1, 2)[None]
    return y.reshape(b, t, D_MODEL), heads(kt), heads(vt), kit.transpose(0, 2, 1)[None], cs[None]


def _sample_layer(x, prev, page_table, cache_k, cache_v, cache_kidx, proj_w, merge_w, *, tm_merge, kc):
    b, t, _ = x.shape
    n_phys = cache_k.shape[0]
    yc, az, gg, cs, q, iq, ikw, k, v = _project(x, prev, *proj_w, tm=t, tokens_on_lanes=False)
    cache_kt = cache_k.transpose(0, 2, 3, 1).reshape(n_phys, D_KV, PAGE_SIZE)
    cache_vt = cache_v.transpose(0, 2, 3, 1).reshape(n_phys, D_KV, PAGE_SIZE)
    cache_kit = cache_kidx.transpose(0, 2, 1)
    attn = _attn_sample(page_table, q, iq, ikw, k, v, cache_kt, cache_vt, cache_kit, kc=kc)
    flat = lambda a: a.reshape(b * t, a.shape[-1])
    y = _merge(flat(x), flat(yc), flat(attn), flat(az), flat(gg), *merge_w, tm=tm_merge)
    heads = lambda a: a.reshape(1, b, t, N_KV_HEADS, HEAD_DIM)
    return y.reshape(b, t, D_MODEL), heads(k), heads(v), ikw[None, :, :, :IDX_DIM], cs[None]


def kernel(x_prompt, x_sample, cache_k, cache_v, cache_kidx, state_conv, page_table, norm_gain, w_in,
           conv_w, w_up_conv, w_up_attn, w_out, final_gain):
    assert norm_gain.shape[0] == 1, "single-layer step"
    proj_w, merge_w = _prepare_weights(norm_gain[0], w_in[0], conv_w[0], w_up_conv[0], w_up_attn[0],
                                       w_out[0], final_gain)
    yp, kp, vp, ip, cp = _prompt_layer(x_prompt, proj_w, merge_w, tm_proj=512, tm_merge=512, tq=256, cks=512)
    ys, ks, vs, is_, cs = _sample_layer(x_sample, state_conv[0], page_table, cache_k[0], cache_v[0],
                                        cache_kidx[0], proj_w, merge_w, tm_merge=512, kc=1024)
    return (yp, ys, kp, vp, ip, cp, ks, vs, is_, cs)
```

```python
import functools

import jax
import jax.numpy as jnp
from jax import lax
from jax.experimental import pallas as pl
from jax.experimental.pallas import tpu as pltpu

D_MODEL = 1024
D_CONV = 512
CONV_WIDTH = 3
N_HEADS = 8
HEAD_DIM = 64
N_KV_HEADS = 2
GROUP = N_HEADS // N_KV_HEADS
D_ATT = N_HEADS * HEAD_DIM
D_KV = N_KV_HEADS * HEAD_DIM
N_IDX_HEADS = 8
IDX_DIM = 64
D_IQ = N_IDX_HEADS * IDX_DIM
TOPK_MAX = 256
PAGE_SIZE = 128
EPS = 1e-6

LANES = 128
SUBLANES = 8
VMEM_LIMIT = 48 * 1024 * 1024
MXU_DTYPE = jnp.bfloat16
GATE_DTYPE = jnp.bfloat16

C_CONV = 0
C_Q = 4 * D_CONV
C_K = C_Q + D_ATT
C_V = C_K + D_KV
C_AZ = C_V + D_KV
C_IQ = C_AZ + D_ATT
C_IKW = C_IQ + D_IQ
C_G = C_IKW + LANES
D_IN_RAW = C_IKW + IDX_DIM + N_IDX_HEADS + 2 * D_MODEL
D_IN_PAD = C_G + 2 * D_MODEL
R_Q = 0
R_IQ = R_Q + D_ATT
R_K = R_IQ + D_IQ
R_V = R_K + D_KV
R_IKW = R_V + D_KV
D_T = R_IKW + IDX_DIM + N_IDX_HEADS

MASKED_DIST = -1e33
MASKED_LOGIT = -1e30
LOG2E = 1.4426950408889634
N_LOG2E_PIECES = 4
POS_RADIX = 64
KEY_BLOCK = 2 * LANES
INT_MIN = -2 ** 31
KEY_NEG_INF = INT_MIN + 0x7FFFFF


def _dot(a, b):
    return jnp.dot(a, b, preferred_element_type=jnp.float32)


def _dot_nt(a, b):
    return lax.dot_general(a, b, (((1,), (1,)), ((), ())), preferred_element_type=jnp.float32)


def _mxu(x):
    return x.astype(MXU_DTYPE)


def _proj_kernel(x_ref, prev_ref, g_ref, w_ref, wt_ref, cw_ref, *refs, tm, tokens_on_lanes):
    yc_ref, az_ref, gg_ref, cs_ref = refs[:4]
    u_buf = refs[-1]
    t = pl.program_id(1)
    nt = pl.num_programs(1)
    x = x_ref[0]
    xn = x * lax.rsqrt(jnp.mean(x * x, axis=-1, keepdims=True) + EPS) * g_ref[...]
    xn = _mxu(xn)

    def proj(c0, width):
        return _dot(xn, w_ref[:, c0:c0 + width])

    def proj_t(r0, height):
        return _dot_nt(wt_ref[r0:r0 + height, :], xn)

    @pl.when(t == 0)
    def _():
        u_buf[0:SUBLANES, :] = jnp.zeros((SUBLANES, D_CONV), jnp.float32)
        u_buf[SUBLANES - 2:SUBLANES, :] = prev_ref[0]

    @pl.when(t > 0)
    def _():
        u_buf[0:SUBLANES, :] = u_buf[tm:tm + SUBLANES, :]

    u_buf[SUBLANES:SUBLANES + tm, :] = proj(C_CONV + D_CONV, D_CONV) * proj(C_CONV + 2 * D_CONV, D_CONV)
    conv = (cw_ref[0:1, :] * u_buf[SUBLANES - 2:SUBLANES - 2 + tm, :]
            + cw_ref[1:2, :] * u_buf[SUBLANES - 1:SUBLANES - 1 + tm, :]
            + cw_ref[2:3, :] * u_buf[SUBLANES:SUBLANES + tm, :])
    yc_ref[0] = (proj(C_CONV, D_CONV) * conv * jax.nn.silu(proj(C_CONV + 3 * D_CONV, D_CONV))).astype(yc_ref.dtype)

    @pl.when(t == nt - 1)
    def _():
        cs_ref[0] = u_buf[tm + SUBLANES - 2:tm + SUBLANES, :]

    az_ref[0] = proj(C_AZ, D_ATT)
    gg_ref[0] = proj(C_G, 2 * D_MODEL).astype(gg_ref.dtype)
    if tokens_on_lanes:
        qt_ref, iqt_ref, kt_ref, vt_ref, kit_ref, iwt_ref, kr_ref, ikwr_ref, vtc_ref = refs[4:13]
        qt_ref[0] = _mxu(proj_t(R_Q, D_ATT) * (HEAD_DIM ** -0.5 * LOG2E))
        iqt_ref[0] = _mxu(proj_t(R_IQ, D_IQ) * (IDX_DIM ** -0.5))
        kt_ref[0] = proj_t(R_K, D_KV)
        vt = proj_t(R_V, D_KV)
        vt_ref[0] = vt
        for s in range(tm // KEY_BLOCK):
            vtc_ref[0, s] = _mxu(vt[:, s * KEY_BLOCK:(s + 1) * KEY_BLOCK])
        kit_ref[0] = proj_t(R_IKW, IDX_DIM)
        iwt_ref[0] = proj_t(R_IKW + IDX_DIM, N_IDX_HEADS)
        kr_ref[0, :, 0:D_KV] = _mxu(proj(C_K, D_KV))
        pos = t * tm + lax.broadcasted_iota(jnp.int32, (tm, LANES), 0)
        lane = lax.broadcasted_iota(jnp.int32, (tm, LANES), 1)
        feat = jnp.where(lane % 2 == 0, pos // POS_RADIX, pos % POS_RADIX)
        kr_ref[0, :, D_KV:D_KV + LANES] = _mxu(
            jnp.where(lane < 2 * N_LOG2E_PIECES, feat, 0).astype(jnp.float32))
        ikwr_ref[0] = _mxu(proj(C_IKW, LANES))
    else:
        q_ref, iq_ref, ikw_ref, k_ref, v_ref = refs[4:9]
        q_ref[0] = proj(C_Q, D_ATT) * (HEAD_DIM ** -0.5)
        iq_ref[0] = proj(C_IQ, D_IQ) * (IDX_DIM ** -0.5)
        ikw_ref[0] = proj(C_IKW, LANES)
        k_ref[0] = proj(C_K, D_KV)
        v_ref[0] = proj(C_V, D_KV)


def _project(x, prev, norm_gain, w_pad, w_t, conv_w, *, tm, tokens_on_lanes):
    b, t, _ = x.shape
    nt = t // tm
    row = lambda width: pl.BlockSpec((1, tm, width), lambda i, j: (i, j, 0))
    col = lambda height: pl.BlockSpec((1, height, tm), lambda i, j: (i, 0, j))
    full = lambda shape: pl.BlockSpec(shape, lambda i, j: (0,) * len(shape), pipeline_mode=pl.Buffered(1))
    state = pl.BlockSpec((1, CONV_WIDTH - 1, D_CONV), lambda i, j: (i, 0, 0))
    f32 = jnp.float32
    sds = jax.ShapeDtypeStruct
    out_shapes = [sds((b, t, D_CONV), MXU_DTYPE), sds((b, t, D_ATT), f32), sds((b, t, 2 * D_MODEL), GATE_DTYPE),
                  sds((b, CONV_WIDTH - 1, D_CONV), f32)]
    out_specs = [row(D_CONV), row(D_ATT), row(2 * D_MODEL), state]
    if tokens_on_lanes:
        out_shapes += [sds((b, D_ATT, t), MXU_DTYPE), sds((b, D_IQ, t), MXU_DTYPE),
                       sds((b, D_KV, t), f32), sds((b, D_KV, t), f32), sds((b, IDX_DIM, t), f32),
                       sds((b, N_IDX_HEADS, t), f32),
                       sds((b, t, D_KV + LANES), MXU_DTYPE), sds((b, t, LANES), MXU_DTYPE),
                       sds((b, t // KEY_BLOCK, D_KV, KEY_BLOCK), MXU_DTYPE)]
        out_specs += [col(D_ATT), col(D_IQ), col(D_KV), col(D_KV), col(IDX_DIM), col(N_IDX_HEADS),
                      row(D_KV + LANES), row(LANES),
                      pl.BlockSpec((1, tm // KEY_BLOCK, D_KV, KEY_BLOCK), lambda i, j: (i, j, 0, 0))]
    else:
        out_shapes += [sds((b, t, D_ATT), f32), sds((b, t, D_IQ), f32), sds((b, t, LANES), f32),
                       sds((b, t, D_KV), f32), sds((b, t, D_KV), f32)]
        out_specs += [row(D_ATT), row(D_IQ), row(LANES), row(D_KV), row(D_KV)]
    return pl.pallas_call(
        functools.partial(_proj_kernel, tm=tm, tokens_on_lanes=tokens_on_lanes),
        grid=(b, nt),
        in_specs=[row(D_MODEL), state, full((1, D_MODEL)), full((D_MODEL, D_IN_PAD)),
                  full((D_T, D_MODEL)), full((CONV_WIDTH, D_CONV))],
        out_specs=out_specs,
        out_shape=out_shapes,
        scratch_shapes=[pltpu.VMEM((tm + SUBLANES, D_CONV), f32)],
        compiler_params=pltpu.CompilerParams(
            dimension_semantics=("arbitrary", "arbitrary"), vmem_limit_bytes=VMEM_LIMIT),
        name="proj",
    )(x, prev, norm_gain.reshape(1, D_MODEL), w_pad, w_t, conv_w)


def _merge_kernel(x_ref, yc_ref, at_ref, az_ref, gg_ref, wuc_ref, wua_ref, wo_ref, fg_ref, y_ref):
    ya = _mxu(at_ref[...] * jax.nn.silu(az_ref[...]))
    gc = gg_ref[:, 0:D_MODEL].astype(jnp.float32)
    ga = gg_ref[:, D_MODEL:2 * D_MODEL].astype(jnp.float32)
    m = (jax.nn.sigmoid(gc) * _dot(yc_ref[...], wuc_ref[...])
         + jax.nn.sigmoid(ga) * _dot(ya, wua_ref[...]))
    xo = x_ref[...] + _dot(_mxu(m), wo_ref[...])
    y_ref[...] = xo * lax.rsqrt(jnp.mean(xo * xo, axis=-1, keepdims=True) + EPS) * fg_ref[...]


def _merge(x, yc, attn, az, gg, wuc, wua, wo, final_gain, *, tm):
    n = x.shape[0]
    row = lambda width: pl.BlockSpec((tm, width), lambda i: (i, 0))
    full = lambda shape: pl.BlockSpec(shape, lambda i: (0,) * len(shape), pipeline_mode=pl.Buffered(1))
    return pl.pallas_call(
        _merge_kernel,
        grid=(n // tm,),
        in_specs=[row(D_MODEL), row(D_CONV), row(D_ATT), row(D_ATT), row(2 * D_MODEL),
                  full((D_CONV, D_MODEL)), full((D_ATT, D_MODEL)), full((D_MODEL, D_MODEL)),
                  full((1, D_MODEL))],
        out_specs=row(D_MODEL),
        out_shape=jax.ShapeDtypeStruct((n, D_MODEL), jnp.float32),
        compiler_params=pltpu.CompilerParams(
            dimension_semantics=("arbitrary",), vmem_limit_bytes=VMEM_LIMIT),
        name="merge",
    )(x, yc, attn, az, gg, wuc, wua, wo, final_gain.reshape(1, D_MODEL))


def _key_to_threshold(tu):
    key = tu ^ jnp.int32(INT_MIN)
    bits = key ^ ((key >> 31) & jnp.int32(0x7FFFFFFF))
    thr = lax.bitcast_convert_type(bits, jnp.float32)
    return jnp.where(key < jnp.int32(KEY_NEG_INF), -jnp.inf, thr)


PACKED_DTYPE = jnp.bfloat16
PACKED_ROWS = 2 * SUBLANES


def _high_half(x):
    bits = lax.bitcast_convert_type(x, jnp.int32) & jnp.int32(-65536)
    return lax.bitcast_convert_type(bits, jnp.float32).astype(PACKED_DTYPE)


def _lane_column(x, lane):
    ids = lax.broadcasted_iota(jnp.int32, x.shape, 1)
    return jnp.sum(jnp.where(ids == lane, x, 0.0), axis=1, keepdims=True)


def _swap_halves(x):
    return pltpu.roll(x, LANES // 2, axis=1)


def _head_slope(h):
    return 2.0 ** (-8.0 * (h + 1) / N_HEADS)


def _pack_head_pair(a0, a1, kv_head):
    lo = lax.broadcasted_iota(jnp.int32, a0.shape, 1) < LANES // 2
    if kv_head == 0:
        return jnp.where(lo, a0, _swap_halves(a1))
    return jnp.where(lo, _swap_halves(a0), a1)


N_COUNT_ACC = 4
EXIT_CHECK_STEPS = 4

def _alibi_rows(tq):
    pieces, rest = [], jnp.float32(LOG2E)
    for _ in range(N_LOG2E_PIECES):
        piece = rest.astype(MXU_DTYPE).astype(jnp.float32)
        pieces.append(piece)
        rest = rest - piece
    rows = jnp.zeros((N_HEADS, LANES), jnp.float32)
    for h in range(N_HEADS):
        for j, piece in enumerate(pieces):
            rows = rows.at[h, 2 * j].set(_head_slope(h) * POS_RADIX * piece)
            rows = rows.at[h, 2 * j + 1].set(_head_slope(h) * piece)
    return _mxu(jnp.broadcast_to(rows[:, :, None], (N_HEADS, LANES, tq)))


def _attn_prompt_kernel(qt_ref, iqt_ref, iwt_ref, kr_ref, ikwr_ref, vtc_ref, aw_ref, o_ref,
                        qw_s, iqw_s, sc_s, sh_s, mask_s, m_s, l_s, a_s, acc_s, j_s, ot_s, s_s, p_s,
                        *, seq, tq, cks, k_sel):
    i = pl.program_id(1)
    n_kb = (i + 1) * (tq // KEY_BLOCK)
    n_cks = ((i + 1) * tq + cks - 1) // cks
    f32 = jnp.float32

    zeros = jnp.zeros((HEAD_DIM, tq), MXU_DTYPE)
    for h in range(N_HEADS):
        g = h // GROUP
        qw_s[h, g * HEAD_DIM:(g + 1) * HEAD_DIM, :] = qt_ref[0, h * HEAD_DIM:(h + 1) * HEAD_DIM, :]
        qw_s[h, (1 - g) * HEAD_DIM:(2 - g) * HEAD_DIM, :] = zeros
        qw_s[h, D_KV:D_KV + LANES, :] = aw_ref[h]
        iqw_s[h, 0:IDX_DIM, :] = iqt_ref[0, h * IDX_DIM:(h + 1) * IDX_DIM, :]
        iqw_s[h, IDX_DIM:2 * IDX_DIM, :] = zeros

    w_rows = iwt_ref[0] * (N_IDX_HEADS ** -0.5)
    q_pos_c = i * tq + lax.broadcasted_iota(jnp.int32, (cks, tq), 1)
    k_row_c = lax.broadcasted_iota(jnp.int32, (cks, tq), 0)

    def score_chunk(c, carry):
        kk = ikwr_ref[0, pl.ds(pl.multiple_of(c * cks, cks), cks), :]
        acc = jnp.zeros((cks, tq), f32)
        for h in range(N_IDX_HEADS):
            acc = acc + jnp.maximum(_dot(kk, iqw_s[h]), 0.0) * w_rows[h:h + 1, :]
        acc = jnp.where(c * cks + k_row_c <= q_pos_c, acc, -jnp.inf)
        for t in range(cks // KEY_BLOCK):
            blk = acc[t * KEY_BLOCK:(t + 1) * KEY_BLOCK]
            sc_s[c * (cks // KEY_BLOCK) + t] = blk
            sh_s[c * (cks // KEY_BLOCK) + t] = _high_half(blk)
        return carry
    lax.fori_loop(0, n_cks, score_chunk, 0)

    sub_id = lax.broadcasted_iota(jnp.int32, (SUBLANES, tq), 0)

    def count(pred):
        def body(j, parts):
            parts = list(parts)
            for r in range(KEY_BLOCK // SUBLANES):
                rows = sc_s[j, r * SUBLANES:(r + 1) * SUBLANES, :]
                hit = pred(rows, j * KEY_BLOCK + r * SUBLANES + sub_id)
                parts[r % N_COUNT_ACC] = parts[r % N_COUNT_ACC] + jnp.where(hit, 1.0, 0.0)
            return tuple(parts)
        parts = lax.fori_loop(0, n_kb, body, (jnp.zeros((SUBLANES, tq), f32),) * N_COUNT_ACC)
        return jnp.sum(functools.reduce(lambda a, b: a + b, parts), axis=0, keepdims=True)

    def rows8(row):
        return jnp.broadcast_to(row, (SUBLANES, tq))

    def count_high(thr_high):
        one = jnp.ones((PACKED_ROWS, tq), PACKED_DTYPE)
        zero = jnp.zeros((PACKED_ROWS, tq), PACKED_DTYPE)
        def body(j, parts):
            parts = list(parts)
            for r in range(KEY_BLOCK // PACKED_ROWS):
                rows = sh_s[j, r * PACKED_ROWS:(r + 1) * PACKED_ROWS, :]
                parts[r % N_COUNT_ACC] = parts[r % N_COUNT_ACC] + jnp.where(rows >= thr_high, one, zero)
            return tuple(parts)
        parts = lax.fori_loop(0, n_kb, body, (zero,) * N_COUNT_ACC)
        total = functools.reduce(lambda a, b: a + b, [p.astype(f32) for p in parts])
        return jnp.sum(total, axis=0, keepdims=True)

    def bisect(it, carry, count_ge):
        tu, cnt_acc = carry
        cand = tu | jnp.left_shift(jnp.int32(1), 31 - it)
        cnt = count_ge(_key_to_threshold(cand))
        ok = cnt >= float(k_sel)
        return jnp.where(ok, cand, tu), jnp.where(ok, cnt, cnt_acc)

    def count_ge_high(thr_c):
        return count_high(jnp.broadcast_to(_high_half(thr_c), (PACKED_ROWS, tq)))

    def count_ge_full(thr_c):
        thr8 = rows8(thr_c)
        return count(lambda s, kp: s >= thr8)

    n_all = (n_kb * KEY_BLOCK).astype(f32)
    carry = (jnp.zeros((1, tq), jnp.int32), jnp.zeros((1, tq), f32) + n_all)
    carry = lax.fori_loop(0, 16, functools.partial(bisect, count_ge=count_ge_high), carry)

    def unresolved(state):
        it, _, cnt = state
        return (it < 32) & (jnp.max(jnp.abs(cnt - float(k_sel))) > 0.0)

    def low_steps(state):
        it, tu, cnt = state
        tu, cnt = lax.fori_loop(it, it + EXIT_CHECK_STEPS,
                                functools.partial(bisect, count_ge=count_ge_full), (tu, cnt))
        return it + EXIT_CHECK_STEPS, tu, cnt

    _, tu, cnt_ge = lax.while_loop(unresolved, low_steps, (jnp.int32(16),) + carry)
    thr = _key_to_threshold(tu)

    j_s[...] = jnp.full((1, tq), seq, jnp.int32)

    @pl.when(jnp.max(cnt_ge) > float(k_sel))
    def _():
        thr8 = rows8(thr)
        need = float(k_sel) - count(lambda s, kp: s > thr8)
        nbits = (seq - 1).bit_length()
        def index_bit(it, p):
            cand = p + jnp.left_shift(jnp.int32(1), nbits - 1 - it)
            cand8 = rows8(cand)
            below = count(lambda s, kp: (s == thr8) & (kp < cand8))
            return jnp.where(below < need, cand, p)
        j_s[...] = lax.fori_loop(0, nbits, index_bit, jnp.zeros((1, tq), jnp.int32))

    m_s[...] = jnp.full(m_s.shape, -jnp.inf, f32)
    l_s[...] = jnp.zeros(l_s.shape, f32)
    acc_s[...] = jnp.zeros(acc_s.shape, f32)
    j_last = j_s[...]
    q_pos = i * tq + lax.broadcasted_iota(jnp.int32, (KEY_BLOCK, tq), 1)
    k_row = lax.broadcasted_iota(jnp.int32, (KEY_BLOCK, tq), 0)

    def attend_block(j, buf):
        k_pos = j * KEY_BLOCK + k_row
        s_idx = sc_s[j]
        sel = ((s_idx > thr) | ((s_idx == thr) & (k_pos <= j_last))) & (k_pos <= q_pos)
        mask_s[buf] = jnp.where(sel, 0.0, MASKED_LOGIT)
        kk = kr_ref[0, pl.ds(pl.multiple_of(j * KEY_BLOCK, KEY_BLOCK), KEY_BLOCK), :]
        vt = vtc_ref[0, j]
        for h in range(N_HEADS):
            s_s[buf, h] = _dot(kk, qw_s[h]) + mask_s[buf]
        for h in range(N_HEADS):
            m_old = m_s[h:h + 1, :]
            m_new = jnp.maximum(m_old, jnp.max(s_s[buf, h], axis=0, keepdims=True))
            alpha = jnp.exp2(m_old - m_new)
            p = jnp.exp2(s_s[buf, h] - m_new)
            l_s[h:h + 1, :] = alpha * l_s[h:h + 1, :] + jnp.sum(p, axis=0, keepdims=True)
            p_s[buf, h] = _mxu(p)
            m_s[h:h + 1, :] = m_new
            a_s[buf, h:h + 1, :] = alpha
        for h in range(N_HEADS):
            g = h // GROUP
            acc_s[h] = (a_s[buf, h:h + 1, :] * acc_s[h]
                        + _dot(vt[g * HEAD_DIM:(g + 1) * HEAD_DIM, :], p_s[buf, h]))

    def attend_pair(t, carry):
        attend_block(2 * t, 0)
        attend_block(2 * t + 1, 1)
        return carry
    lax.fori_loop(0, n_kb // 2, attend_pair, 0)

    @pl.when(n_kb % 2 == 1)
    def _():
        attend_block(n_kb - 1, 0)

    for h in range(N_HEADS):
        ot_s[h * HEAD_DIM:(h + 1) * HEAD_DIM, :] = acc_s[h] / l_s[h:h + 1, :]
    o_ref[0] = ot_s[...].T


def _attn_prompt(qt, iqt, iwt, k_rows, ikw_rows, vt_blocks, *, tq, cks):
    b, _, seq = qt.shape
    n_kb = seq // KEY_BLOCK
    k_sel = min(TOPK_MAX, seq // 4)
    colblk = lambda height: pl.BlockSpec((1, height, tq), lambda i, j: (i, 0, j))
    whole = lambda shape: pl.BlockSpec((1,) + shape, lambda i, j: (i,) + (0,) * len(shape))
    f32 = jnp.float32
    return pl.pallas_call(
        functools.partial(_attn_prompt_kernel, seq=seq, tq=tq, cks=cks, k_sel=k_sel),
        grid=(b, seq // tq),
        in_specs=[colblk(D_ATT), colblk(D_IQ), colblk(N_IDX_HEADS),
                  whole((seq, D_KV + LANES)), whole((seq, LANES)), whole((n_kb, D_KV, KEY_BLOCK)),
                  pl.BlockSpec((N_HEADS, LANES, tq), lambda i, j: (0, 0, 0))],
        out_specs=pl.BlockSpec((1, tq, D_ATT), lambda i, j: (i, j, 0)),
        out_shape=jax.ShapeDtypeStruct((b, seq, D_ATT), f32),
        scratch_shapes=[
            pltpu.VMEM((N_HEADS, D_KV + LANES, tq), MXU_DTYPE),
            pltpu.VMEM((N_IDX_HEADS, LANES, tq), MXU_DTYPE),
            pltpu.VMEM((n_kb, KEY_BLOCK, tq), f32),
            pltpu.VMEM((n_kb, KEY_BLOCK, tq), PACKED_DTYPE),
            pltpu.VMEM((2, KEY_BLOCK, tq), f32),
            pltpu.VMEM((N_HEADS, tq), f32),
            pltpu.VMEM((N_HEADS, tq), f32),
            pltpu.VMEM((2, N_HEADS, tq), f32),
            pltpu.VMEM((N_HEADS, HEAD_DIM, tq), f32),
            pltpu.VMEM((1, tq), jnp.int32),
            pltpu.VMEM((D_ATT, tq), f32),
            pltpu.VMEM((2, N_HEADS, KEY_BLOCK, tq), f32),
            pltpu.VMEM((2, N_HEADS, KEY_BLOCK, tq), MXU_DTYPE),
        ],
        compiler_params=pltpu.CompilerParams(
            dimension_semantics=("arbitrary", "arbitrary"), vmem_limit_bytes=VMEM_LIMIT),
        name="attn_prompt",
    )(qt, iqt, iwt, k_rows, ikw_rows, vt_blocks, _alibi_rows(tq))


def _attn_sample_kernel(pt_ref, q_ref, iq_ref, ikw_ref, k_ref, v_ref, ck_hbm, cv_hbm, ci_hbm, o_ref,
                        kbuf, vbuf, ibuf, sc_s, s_s, sem,
                        *, n_pages, t_new, kc, k_sel):
    b = pl.program_id(0)
    nb = pl.num_programs(0)
    slot = b % 2
    past = n_pages * PAGE_SIZE
    n_rows = N_HEADS * t_new
    f32 = jnp.float32

    def page_copies(seq_id, seq_slot, p):
        phys = pt_ref[seq_id, p]
        cols = pl.ds(pl.multiple_of(p * PAGE_SIZE, PAGE_SIZE), PAGE_SIZE)
        return (pltpu.make_async_copy(ck_hbm.at[phys], kbuf.at[seq_slot, :, cols], sem.at[seq_slot, 0]),
                pltpu.make_async_copy(cv_hbm.at[phys], vbuf.at[seq_slot, :, cols], sem.at[seq_slot, 1]),
                pltpu.make_async_copy(ci_hbm.at[phys], ibuf.at[seq_slot, :, cols], sem.at[seq_slot, 2]))

    def start_pages(seq_id, seq_slot):
        def start_page(p, carry):
            for cp in page_copies(seq_id, seq_slot, p):
                cp.start()
            return carry
        lax.fori_loop(0, n_pages, start_page, 0)

    @pl.when(b == 0)
    def _():
        start_pages(0, 0)

    @pl.when(b + 1 < nb)
    def _():
        start_pages(b + 1, 1 - slot)

    lo = lax.broadcasted_iota(jnp.int32, (t_new, LANES), 1) < LANES // 2
    ikw = ikw_ref[0]
    iq_rows, q_rows, w_rows = [], [], []
    for h in range(N_HEADS):
        it = iq_ref[0, :, (h // 2) * LANES:(h // 2 + 1) * LANES]
        iq_rows.append((it if h % 2 == 0 else _swap_halves(it))[:, 0:IDX_DIM])
        qt = q_ref[0, :, (h // 2) * LANES:(h // 2 + 1) * LANES]
        g, par = h // GROUP, h % 2
        if g != par:
            qt = _swap_halves(qt)
        q_rows.append(jnp.where(lo, qt, 0.0) if g == 0 else jnp.where(lo, 0.0, qt))
        w_rows.append(_lane_column(ikw, IDX_DIM + h) * (N_IDX_HEADS ** -0.5))
    iq_all = _mxu(jnp.concatenate(iq_rows, axis=0))
    q_all = _mxu(jnp.concatenate(q_rows, axis=0))
    w_all = jnp.concatenate(w_rows, axis=0)
    slope = jnp.concatenate(
        [jnp.full((t_new, 1), _head_slope(h), f32) for h in range(N_HEADS)], axis=0)

    pad = jnp.zeros((LANES - t_new, LANES), f32)
    ik_new = _mxu(jnp.concatenate([ikw, pad], axis=0)[:, 0:IDX_DIM])
    k_new = _mxu(jnp.concatenate([k_ref[0], pad], axis=0))
    v_new = _mxu(jnp.concatenate([v_ref[0], pad], axis=0))

    def head_sum(r):
        acc = r[0:t_new]
        for h in range(1, N_HEADS):
            acc = acc + r[h * t_new:(h + 1) * t_new]
        return acc

    def wait_page(p, carry):
        for cp in page_copies(b, slot, p):
            cp.wait()
        return carry
    lax.fori_loop(0, n_pages, wait_page, 0)

    for c in range(past // kc):
        cols = slice(c * kc, (c + 1) * kc)
        sc_s[:, cols] = head_sum(jnp.maximum(_dot(iq_all, _mxu(ibuf[slot, :, cols])), 0.0) * w_all)
    t_id = lax.broadcasted_iota(jnp.int32, (t_new, LANES), 0)
    j_id = lax.broadcasted_iota(jnp.int32, (t_new, LANES), 1)
    s_new = head_sum(jnp.maximum(_dot_nt(iq_all, ik_new), 0.0) * w_all)
    sc_s[:, past:past + LANES] = jnp.where(j_id <= t_id, s_new, -jnp.inf)

    sc = sc_s[...]
    def count(mask):
        return jnp.sum(jnp.where(mask, 1.0, 0.0), axis=1, keepdims=True)

    def bisect(it, carry):
        tu, cnt_acc = carry
        shift = 30 - 2 * it
        for digit in (1, 2, 3):
            cand = tu | jnp.left_shift(jnp.int32(digit), shift)
            cnt = count(sc >= _key_to_threshold(cand))
            ok = cnt >= float(k_sel)
            best = jnp.where(ok, cand, tu if digit == 1 else best)
            cnt_acc = jnp.where(ok, cnt, cnt_acc)
        return best, cnt_acc

    tu, cnt_ge = lax.fori_loop(
        0, 16, bisect,
        (jnp.zeros((t_new, 1), jnp.int32), jnp.full((t_new, 1), float(past + LANES), f32)))
    thr = _key_to_threshold(tu)

    k_pos = lax.broadcasted_iota(jnp.int32, sc.shape, 1)
    q_pos = past + lax.broadcasted_iota(jnp.int32, sc.shape, 0)
    nbits = (past + LANES - 1).bit_length()

    def tie_limit():
        need = float(k_sel) - count(sc > thr)
        def index_bit(it, p):
            cand = p + jnp.left_shift(jnp.int32(1), nbits - 1 - it)
            below = count((sc == thr) & (k_pos < cand))
            return jnp.where(below < need, cand, p)
        return lax.fori_loop(0, nbits, index_bit, jnp.zeros((t_new, 1), jnp.int32))

    j_last = lax.cond(jnp.max(cnt_ge) > float(k_sel), tie_limit,
                      lambda: jnp.full((t_new, 1), past + LANES, jnp.int32))
    sel = ((sc > thr) | ((sc == thr) & (k_pos <= j_last))) & (k_pos <= q_pos)
    dist = jnp.where(sel, (k_pos - q_pos).astype(f32), MASKED_DIST)

    def logits(s, d):
        return s + slope * jnp.concatenate([d] * N_HEADS, axis=0)

    m = jnp.full((n_rows, 1), -jnp.inf, f32)
    for c in range(past // kc):
        cols = slice(c * kc, (c + 1) * kc)
        s = logits(_dot(q_all, _mxu(kbuf[slot, :, cols])), dist[:, cols])
        s_s[:, cols] = s
        m = jnp.maximum(m, jnp.max(s, axis=1, keepdims=True))
    s = logits(_dot_nt(q_all, k_new), dist[:, past:past + LANES])
    s_s[:, past:past + LANES] = s
    m = jnp.maximum(m, jnp.max(s, axis=1, keepdims=True))

    l = jnp.zeros((n_rows, 1), f32)
    o = jnp.zeros((n_rows, LANES), f32)
    for c in range(past // kc):
        cols = slice(c * kc, (c + 1) * kc)
        p = jnp.exp(s_s[:, cols] - m)
        l = l + jnp.sum(p, axis=1, keepdims=True)
        o = o + _dot_nt(_mxu(p), _mxu(vbuf[slot, :, cols]))
    p = jnp.exp(s_s[:, past:past + LANES] - m)
    l = l + jnp.sum(p, axis=1, keepdims=True)
    o = (o + _dot(_mxu(p), v_new)) / l

    for j in range(N_HEADS // 2):
        o_ref[0, :, j * LANES:(j + 1) * LANES] = _pack_head_pair(
            o[(2 * j) * t_new:(2 * j + 1) * t_new], o[(2 * j + 1) * t_new:(2 * j + 2) * t_new],
            (2 * j) // GROUP)


def _attn_sample(page_table, q, iq, ikw, k, v, cache_kt, cache_vt, cache_kit, *, kc):
    db, t_new, _ = q.shape
    n_pages = page_table.shape[1]
    past = n_pages * PAGE_SIZE
    k_sel = min(TOPK_MAX, (past + t_new) // 4)
    blk = lambda width: pl.BlockSpec((1, t_new, width), lambda i, pt: (i, 0, 0))
    hbm = pl.BlockSpec(memory_space=pl.ANY)
    f32 = jnp.float32
    return pl.pallas_call(
        functools.partial(_attn_sample_kernel, n_pages=n_pages, t_new=t_new, kc=kc, k_sel=k_sel),
        grid_spec=pltpu.PrefetchScalarGridSpec(
            num_scalar_prefetch=1,
            grid=(db,),
            in_specs=[blk(D_ATT), blk(D_IQ), blk(LANES), blk(D_KV), blk(D_KV), hbm, hbm, hbm],
            out_specs=blk(D_ATT),
            scratch_shapes=[
                pltpu.VMEM((2, D_KV, past), f32),
                pltpu.VMEM((2, D_KV, past), f32),
                pltpu.VMEM((2, IDX_DIM, past), f32),
                pltpu.VMEM((t_new, past + LANES), f32),
                pltpu.VMEM((N_HEADS * t_new, past + LANES), f32),
                pltpu.SemaphoreType.DMA((2, 3)),
            ]),
        out_shape=jax.ShapeDtypeStruct((db, t_new, D_ATT), f32),
        compiler_params=pltpu.CompilerParams(
            dimension_semantics=("arbitrary",), vmem_limit_bytes=VMEM_LIMIT),
        name="attn_sample",
    )(page_table, q, iq, ikw, k, v, cache_kt, cache_vt, cache_kit)


def _prepare_weights(norm_gain, w_in, conv_w, w_up_conv, w_up_attn, w_out, final_gain):
    split = C_IKW + IDX_DIM + N_IDX_HEADS
    pad = jnp.zeros((D_MODEL, D_IN_PAD - D_IN_RAW), w_in.dtype)
    w_pad = _mxu(jnp.concatenate([w_in[:, :split], pad, w_in[:, split:]], axis=1))
    w_t = _mxu(jnp.concatenate(
        [w_in[:, C_Q:C_Q + D_ATT], w_in[:, C_IQ:C_IQ + D_IQ], w_in[:, C_K:C_K + 2 * D_KV],
         w_in[:, C_IKW:split]], axis=1).T)
    return (norm_gain, w_pad, w_t, conv_w), (_mxu(w_up_conv), _mxu(w_up_attn), _mxu(w_out), final_gain)


def _prompt_layer(x, proj_w, merge_w, *, tm_proj, tm_merge, tq, cks):
    b, t, _ = x.shape
    prev = jnp.zeros((b, CONV_WIDTH - 1, D_CONV), x.dtype)
    yc, az, gg, cs, qt, iqt, kt, vt, kit, iwt, k_rows, ikw_rows, vt_blocks = _project(
        x, prev, *proj_w, tm=tm_proj, tokens_on_lanes=True)
    attn = _attn_prompt(qt, iqt, iwt, k_rows, ikw_rows, vt_blocks, tq=tq, cks=cks)
    flat = lambda a: a.reshape(b * t, a.shape[-1])
    y = _merge(flat(x), flat(yc), flat(attn), flat(az), flat(gg), *merge_w, tm=tm_merge)
    heads = lambda a: a.reshape(b, N_KV_HEADS, HEAD_DIM, t).transpose(0, 3, 1, 2)[None]
    return y.reshape(b, t, D_MODEL), heads(kt), heads(vt), kit.transpose(0, 2, 1)[None], cs[None]


def _sample_layer(x, prev, page_table, cache_k, cache_v, cache_kidx, proj_w, merge_w, *, tm_merge, kc):
    b, t, _ = x.shape
    n_phys = cache_k.shape[0]
    yc, az, gg, cs, q, iq, ikw, k, v = _project(x, prev, *proj_w, tm=t, tokens_on_lanes=False)
    cache_kt = cache_k.transpose(0, 2, 3, 1).reshape(n_phys, D_KV, PAGE_SIZE)
    cache_vt = cache_v.transpose(0, 2, 3, 1).reshape(n_phys, D_KV, PAGE_SIZE)
    cache_kit = cache_kidx.transpose(0, 2, 1)
    attn = _attn_sample(page_table, q, iq, ikw, k, v, cache_kt, cache_vt, cache_kit, kc=kc)
    flat = lambda a: a.reshape(b * t, a.shape[-1])
    y = _merge(flat(x), flat(yc), flat(attn), flat(az), flat(gg), *merge_w, tm=tm_merge)
    heads = lambda a: a.reshape(1, b, t, N_KV_HEADS, HEAD_DIM)
    return y.reshape(b, t, D_MODEL), heads(k), heads(v), ikw[None, :, :, :IDX_DIM], cs[None]


def kernel(x_prompt, x_sample, cache_k, cache_v, cache_kidx, state_conv, page_table, norm_gain, w_in,
           conv_w, w_up_conv, w_up_attn, w_out, final_gain):
    assert norm_gain.shape[0] == 1, "single-layer step"
    proj_w, merge_w = _prepare_weights(norm_gain[0], w_in[0], conv_w[0], w_up_conv[0], w_up_attn[0],
                                       w_out[0], final_gain)
    yp, kp, vp, ip, cp = _prompt_layer(x_prompt, proj_w, merge_w, tm_proj=512, tm_merge=512, tq=256, cks=512)
    ys, ks, vs, is_, cs = _sample_layer(x_sample, state_conv[0], page_table, cache_k[0], cache_v[0],
                                        cache_kidx[0], proj_w, merge_w, tm_merge=512, kc=1024)
    return (yp, ys, kp, vp, ip, cp, ks, vs, is_, cs)
```

```python
import functools

import jax
import jax.numpy as jnp
from jax import lax
from jax.experimental import pallas as pl
from jax.experimental.pallas import tpu as pltpu

D_MODEL = 1024
D_CONV = 512
CONV_WIDTH = 3
N_HEADS = 8
HEAD_DIM = 64
N_KV_HEADS = 2
GROUP = N_HEADS // N_KV_HEADS
D_ATT = N_HEADS * HEAD_DIM
D_KV = N_KV_HEADS * HEAD_DIM
N_IDX_HEADS = 8
IDX_DIM = 64
D_IQ = N_IDX_HEADS * IDX_DIM
TOPK_MAX = 256
PAGE_SIZE = 128
EPS = 1e-6

LANES = 128
SUBLANES = 8
VMEM_LIMIT = 48 * 1024 * 1024
MXU_DTYPE = jnp.bfloat16
GATE_DTYPE = jnp.bfloat16

C_CONV = 0
C_Q = 4 * D_CONV
C_K = C_Q + D_ATT
C_V = C_K + D_KV
C_AZ = C_V + D_KV
C_IQ = C_AZ + D_ATT
C_IKW = C_IQ + D_IQ
C_G = C_IKW + LANES
D_IN_RAW = C_IKW + IDX_DIM + N_IDX_HEADS + 2 * D_MODEL
D_IN_PAD = C_G + 2 * D_MODEL
R_Q = 0
R_IQ = R_Q + D_ATT
R_K = R_IQ + D_IQ
R_V = R_K + D_KV
R_IKW = R_V + D_KV
D_T = R_IKW + IDX_DIM + N_IDX_HEADS

MASKED_DIST = -1e33
MASKED_LOGIT = -1e30
LOG2E = 1.4426950408889634
N_LOG2E_PIECES = 4
POS_RADIX = 64
KEY_BLOCK = 2 * LANES
INT_MIN = -2 ** 31
KEY_NEG_INF = INT_MIN + 0x7FFFFF


def _dot(a, b):
    return jnp.dot(a, b, preferred_element_type=jnp.float32)


def _dot_nt(a, b):
    return lax.dot_general(a, b, (((1,), (1,)), ((), ())), preferred_element_type=jnp.float32)


def _mxu(x):
    return x.astype(MXU_DTYPE)


def _proj_kernel(x_ref, prev_ref, g_ref, w_ref, wt_ref, cw_ref, *refs, tm, tokens_on_lanes, rows_per_seq):
    yc_ref, az_ref, gg_ref, cs_ref = refs[:4]
    u_buf = refs[-1]
    t = pl.program_id(1)
    nt = pl.num_programs(1)
    x = x_ref[0]
    xn = x * lax.rsqrt(jnp.mean(x * x, axis=-1, keepdims=True) + EPS) * g_ref[...]
    xn = _mxu(xn)

    def proj(c0, width):
        return _dot(xn, w_ref[:, c0:c0 + width])

    def proj_t(r0, height):
        return _dot_nt(wt_ref[r0:r0 + height, :], xn)

    u = proj(C_CONV + D_CONV, D_CONV) * proj(C_CONV + 2 * D_CONV, D_CONV)
    if rows_per_seq is None:
        @pl.when(t == 0)
        def _():
            u_buf[0:SUBLANES, :] = jnp.zeros((SUBLANES, D_CONV), jnp.float32)
            u_buf[SUBLANES - 2:SUBLANES, :] = prev_ref[0]

        @pl.when(t > 0)
        def _():
            u_buf[0:SUBLANES, :] = u_buf[tm:tm + SUBLANES, :]

        u_buf[SUBLANES:SUBLANES + tm, :] = u
        u_m1 = u_buf[SUBLANES - 1:SUBLANES - 1 + tm, :]
        u_m2 = u_buf[SUBLANES - 2:SUBLANES - 2 + tm, :]

        @pl.when(t == nt - 1)
        def _():
            cs_ref[0] = u_buf[tm + SUBLANES - 2:tm + SUBLANES, :]
    else:
        u_buf[0:SUBLANES, :] = jnp.zeros((SUBLANES, D_CONV), jnp.float32)
        u_buf[SUBLANES:SUBLANES + tm, :] = u
        row_in_seq = lax.broadcasted_iota(jnp.int32, (tm, D_CONV), 0) % rows_per_seq
        u_m1 = jnp.where(row_in_seq >= 1, u_buf[SUBLANES - 1:SUBLANES - 1 + tm, :], 0.0) + prev_ref[0, 0]
        u_m2 = jnp.where(row_in_seq >= 2, u_buf[SUBLANES - 2:SUBLANES - 2 + tm, :], 0.0) + prev_ref[0, 1]
        cs_ref[0] = u
    conv = cw_ref[0:1, :] * u_m2 + cw_ref[1:2, :] * u_m1 + cw_ref[2:3, :] * u
    yc_ref[0] = (proj(C_CONV, D_CONV) * conv * jax.nn.silu(proj(C_CONV + 3 * D_CONV, D_CONV))).astype(yc_ref.dtype)

    az_ref[0] = proj(C_AZ, D_ATT)
    gg_ref[0] = proj(C_G, 2 * D_MODEL).astype(gg_ref.dtype)
    if tokens_on_lanes:
        qt_ref, iqt_ref, kt_ref, vt_ref, kit_ref, iwt_ref, kr_ref, ikwr_ref, vtc_ref = refs[4:13]
        qt_ref[0] = _mxu(proj_t(R_Q, D_ATT) * (HEAD_DIM ** -0.5 * LOG2E))
        iqt_ref[0] = _mxu(proj_t(R_IQ, D_IQ) * (IDX_DIM ** -0.5))
        kt_ref[0] = proj_t(R_K, D_KV)
        vt = proj_t(R_V, D_KV)
        vt_ref[0] = vt
        for s in range(tm // KEY_BLOCK):
            vtc_ref[0, s] = _mxu(vt[:, s * KEY_BLOCK:(s + 1) * KEY_BLOCK])
        kit_ref[0] = proj_t(R_IKW, IDX_DIM)
        iwt_ref[0] = proj_t(R_IKW + IDX_DIM, N_IDX_HEADS)
        kr_ref[0, :, 0:D_KV] = _mxu(proj(C_K, D_KV))
        pos = t * tm + lax.broadcasted_iota(jnp.int32, (tm, LANES), 0)
        lane = lax.broadcasted_iota(jnp.int32, (tm, LANES), 1)
        feat = jnp.where(lane % 2 == 0, pos // POS_RADIX, pos % POS_RADIX)
        kr_ref[0, :, D_KV:D_KV + LANES] = _mxu(
            jnp.where(lane < 2 * N_LOG2E_PIECES, feat, 0).astype(jnp.float32))
        ikwr_ref[0] = _mxu(proj(C_IKW, LANES))
    else:
        q_ref, iq_ref, ikw_ref, k_ref, v_ref = refs[4:9]
        q_ref[0] = proj(C_Q, D_ATT) * (HEAD_DIM ** -0.5)
        iq_ref[0] = proj(C_IQ, D_IQ) * (IDX_DIM ** -0.5)
        ikw_ref[0] = proj(C_IKW, LANES)
        k_ref[0] = proj(C_K, D_KV)
        v_ref[0] = proj(C_V, D_KV)


def _project(x, prev, norm_gain, w_pad, w_t, conv_w, *, tm, tokens_on_lanes, rows_per_seq=None):
    b, t, _ = x.shape
    nt = t // tm
    row = lambda width: pl.BlockSpec((1, tm, width), lambda i, j: (i, j, 0))
    col = lambda height: pl.BlockSpec((1, height, tm), lambda i, j: (i, 0, j))
    full = lambda shape: pl.BlockSpec(shape, lambda i, j: (0,) * len(shape), pipeline_mode=pl.Buffered(1))
    f32 = jnp.float32
    sds = jax.ShapeDtypeStruct
    if rows_per_seq is None:
        state_in = state_out = pl.BlockSpec((1, CONV_WIDTH - 1, D_CONV), lambda i, j: (i, 0, 0))
        state_shape = sds((b, CONV_WIDTH - 1, D_CONV), f32)
    else:
        assert tm == t and t % rows_per_seq == 0 and rows_per_seq >= CONV_WIDTH - 1
        state_in = pl.BlockSpec((1, CONV_WIDTH - 1, t, D_CONV), lambda i, j: (i, 0, 0, 0))
        state_out, state_shape = row(D_CONV), sds((b, t, D_CONV), f32)
    out_shapes = [sds((b, t, D_CONV), MXU_DTYPE), sds((b, t, D_ATT), f32), sds((b, t, 2 * D_MODEL), GATE_DTYPE),
                  state_shape]
    out_specs = [row(D_CONV), row(D_ATT), row(2 * D_MODEL), state_out]
    if tokens_on_lanes:
        out_shapes += [sds((b, D_ATT, t), MXU_DTYPE), sds((b, D_IQ, t), MXU_DTYPE),
                       sds((b, D_KV, t), f32), sds((b, D_KV, t), f32), sds((b, IDX_DIM, t), f32),
                       sds((b, N_IDX_HEADS, t), f32),
                       sds((b, t, D_KV + LANES), MXU_DTYPE), sds((b, t, LANES), MXU_DTYPE),
                       sds((b, t // KEY_BLOCK, D_KV, KEY_BLOCK), MXU_DTYPE)]
        out_specs += [col(D_ATT), col(D_IQ), col(D_KV), col(D_KV), col(IDX_DIM), col(N_IDX_HEADS),
                      row(D_KV + LANES), row(LANES),
                      pl.BlockSpec((1, tm // KEY_BLOCK, D_KV, KEY_BLOCK), lambda i, j: (i, j, 0, 0))]
    else:
        out_shapes += [sds((b, t, D_ATT), f32), sds((b, t, D_IQ), f32), sds((b, t, LANES), f32),
                       sds((b, t, D_KV), f32), sds((b, t, D_KV), f32)]
        out_specs += [row(D_ATT), row(D_IQ), row(LANES), row(D_KV), row(D_KV)]
    return pl.pallas_call(
        functools.partial(_proj_kernel, tm=tm, tokens_on_lanes=tokens_on_lanes, rows_per_seq=rows_per_seq),
        grid=(b, nt),
        in_specs=[row(D_MODEL), state_in, full((1, D_MODEL)), full((D_MODEL, D_IN_PAD)),
                  full((D_T, D_MODEL)), full((CONV_WIDTH, D_CONV))],
        out_specs=out_specs,
        out_shape=out_shapes,
        scratch_shapes=[pltpu.VMEM((tm + SUBLANES, D_CONV), f32)],
        compiler_params=pltpu.CompilerParams(
            dimension_semantics=("arbitrary", "arbitrary"), vmem_limit_bytes=VMEM_LIMIT),
        name="proj",
    )(x, prev, norm_gain.reshape(1, D_MODEL), w_pad, w_t, conv_w)


def _merge_kernel(x_ref, yc_ref, at_ref, az_ref, gg_ref, wuc_ref, wua_ref, wo_ref, fg_ref, y_ref):
    ya = _mxu(at_ref[...] * jax.nn.silu(az_ref[...]))
    gc = gg_ref[:, 0:D_MODEL].astype(jnp.float32)
    ga = gg_ref[:, D_MODEL:2 * D_MODEL].astype(jnp.float32)
    m = (jax.nn.sigmoid(gc) * _dot(yc_ref[...], wuc_ref[...])
         + jax.nn.sigmoid(ga) * _dot(ya, wua_ref[...]))
    xo = x_ref[...] + _dot(_mxu(m), wo_ref[...])
    y_ref[...] = xo * lax.rsqrt(jnp.mean(xo * xo, axis=-1, keepdims=True) + EPS) * fg_ref[...]


def _merge(x, yc, attn, az, gg, wuc, wua, wo, final_gain, *, tm):
    n = x.shape[0]
    row = lambda width: pl.BlockSpec((tm, width), lambda i: (i, 0))
    full = lambda shape: pl.BlockSpec(shape, lambda i: (0,) * len(shape), pipeline_mode=pl.Buffered(1))
    return pl.pallas_call(
        _merge_kernel,
        grid=(n // tm,),
        in_specs=[row(D_MODEL), row(D_CONV), row(D_ATT), row(D_ATT), row(2 * D_MODEL),
                  full((D_CONV, D_MODEL)), full((D_ATT, D_MODEL)), full((D_MODEL, D_MODEL)),
                  full((1, D_MODEL))],
        out_specs=row(D_MODEL),
        out_shape=jax.ShapeDtypeStruct((n, D_MODEL), jnp.float32),
        compiler_params=pltpu.CompilerParams(
            dimension_semantics=("arbitrary",), vmem_limit_bytes=VMEM_LIMIT),
        name="merge",
    )(x, yc, attn, az, gg, wuc, wua, wo, final_gain.reshape(1, D_MODEL))


def _key_to_threshold(tu):
    key = tu ^ jnp.int32(INT_MIN)
    bits = key ^ ((key >> 31) & jnp.int32(0x7FFFFFFF))
    thr = lax.bitcast_convert_type(bits, jnp.float32)
    return jnp.where(key < jnp.int32(KEY_NEG_INF), -jnp.inf, thr)


PACKED_DTYPE = jnp.bfloat16
PACKED_ROWS = 2 * SUBLANES


def _high_half(x):
    bits = lax.bitcast_convert_type(x, jnp.int32) & jnp.int32(-65536)
    return lax.bitcast_convert_type(bits, jnp.float32).astype(PACKED_DTYPE)


def _lane_column(x, lane):
    ids = lax.broadcasted_iota(jnp.int32, x.shape, 1)
    return jnp.sum(jnp.where(ids == lane, x, 0.0), axis=1, keepdims=True)


def _swap_halves(x):
    return pltpu.roll(x, LANES // 2, axis=1)


def _head_slope(h):
    return 2.0 ** (-8.0 * (h + 1) / N_HEADS)


def _pack_head_pair(a0, a1, kv_head):
    lo = lax.broadcasted_iota(jnp.int32, a0.shape, 1) < LANES // 2
    if kv_head == 0:
        return jnp.where(lo, a0, _swap_halves(a1))
    return jnp.where(lo, _swap_halves(a0), a1)


N_COUNT_ACC = 4
EXIT_CHECK_STEPS = 4

def _alibi_rows(tq):
    pieces, rest = [], jnp.float32(LOG2E)
    for _ in range(N_LOG2E_PIECES):
        piece = rest.astype(MXU_DTYPE).astype(jnp.float32)
        pieces.append(piece)
        rest = rest - piece
    rows = jnp.zeros((N_HEADS, LANES), jnp.float32)
    for h in range(N_HEADS):
        for j, piece in enumerate(pieces):
            rows = rows.at[h, 2 * j].set(_head_slope(h) * POS_RADIX * piece)
            rows = rows.at[h, 2 * j + 1].set(_head_slope(h) * piece)
    return _mxu(jnp.broadcast_to(rows[:, :, None], (N_HEADS, LANES, tq)))


def _attn_prompt_kernel(qt_ref, iqt_ref, iwt_ref, kr_ref, ikwr_ref, vtc_ref, aw_ref, o_ref,
                        qw_s, iqw_s, sc_s, sh_s, mask_s, m_s, l_s, a_s, acc_s, j_s, ot_s, s_s, p_s,
                        *, seq, tq, cks, k_sel):
    i = pl.program_id(1)
    n_kb = (i + 1) * (tq // KEY_BLOCK)
    n_cks = ((i + 1) * tq + cks - 1) // cks
    f32 = jnp.float32

    zeros = jnp.zeros((HEAD_DIM, tq), MXU_DTYPE)
    for h in range(N_HEADS):
        g = h // GROUP
        qw_s[h, g * HEAD_DIM:(g + 1) * HEAD_DIM, :] = qt_ref[0, h * HEAD_DIM:(h + 1) * HEAD_DIM, :]
        qw_s[h, (1 - g) * HEAD_DIM:(2 - g) * HEAD_DIM, :] = zeros
        qw_s[h, D_KV:D_KV + LANES, :] = aw_ref[h]
        iqw_s[h, 0:IDX_DIM, :] = iqt_ref[0, h * IDX_DIM:(h + 1) * IDX_DIM, :]
        iqw_s[h, IDX_DIM:2 * IDX_DIM, :] = zeros

    w_rows = iwt_ref[0] * (N_IDX_HEADS ** -0.5)
    q_pos_c = i * tq + lax.broadcasted_iota(jnp.int32, (cks, tq), 1)
    k_row_c = lax.broadcasted_iota(jnp.int32, (cks, tq), 0)

    def score_chunk(c, carry):
        kk = ikwr_ref[0, pl.ds(pl.multiple_of(c * cks, cks), cks), :]
        acc = jnp.zeros((cks, tq), f32)
        for h in range(N_IDX_HEADS):
            acc = acc + jnp.maximum(_dot(kk, iqw_s[h]), 0.0) * w_rows[h:h + 1, :]
        acc = jnp.where(c * cks + k_row_c <= q_pos_c, acc, -jnp.inf)
        for t in range(cks // KEY_BLOCK):
            blk = acc[t * KEY_BLOCK:(t + 1) * KEY_BLOCK]
            sc_s[c * (cks // KEY_BLOCK) + t] = blk
            sh_s[c * (cks // KEY_BLOCK) + t] = _high_half(blk)
        return carry
    lax.fori_loop(0, n_cks, score_chunk, 0)

    sub_id = lax.broadcasted_iota(jnp.int32, (SUBLANES, tq), 0)

    def count(pred):
        def body(j, parts):
            parts = list(parts)
            for r in range(KEY_BLOCK // SUBLANES):
                rows = sc_s[j, r * SUBLANES:(r + 1) * SUBLANES, :]
                hit = pred(rows, j * KEY_BLOCK + r * SUBLANES + sub_id)
                parts[r % N_COUNT_ACC] = parts[r % N_COUNT_ACC] + jnp.where(hit, 1.0, 0.0)
            return tuple(parts)
        parts = lax.fori_loop(0, n_kb, body, (jnp.zeros((SUBLANES, tq), f32),) * N_COUNT_ACC)
        return jnp.sum(functools.reduce(lambda a, b: a + b, parts), axis=0, keepdims=True)

    def rows8(row):
        return jnp.broadcast_to(row, (SUBLANES, tq))

    def count_high(thr_high):
        one = jnp.ones((PACKED_ROWS, tq), PACKED_DTYPE)
        zero = jnp.zeros((PACKED_ROWS, tq), PACKED_DTYPE)
        def body(j, parts):
            parts = list(parts)
            for r in range(KEY_BLOCK // PACKED_ROWS):
                rows = sh_s[j, r * PACKED_ROWS:(r + 1) * PACKED_ROWS, :]
                parts[r % N_COUNT_ACC] = parts[r % N_COUNT_ACC] + jnp.where(rows >= thr_high, one, zero)
            return tuple(parts)
        parts = lax.fori_loop(0, n_kb, body, (zero,) * N_COUNT_ACC)
        total = functools.reduce(lambda a, b: a + b, [p.astype(f32) for p in parts])
        return jnp.sum(total, axis=0, keepdims=True)

    def bisect(it, carry, count_ge):
        tu, cnt_acc = carry
        cand = tu | jnp.left_shift(jnp.int32(1), 31 - it)
        cnt = count_ge(_key_to_threshold(cand))
        ok = cnt >= float(k_sel)
        return jnp.where(ok, cand, tu), jnp.where(ok, cnt, cnt_acc)

    def count_ge_high(thr_c):
        return count_high(jnp.broadcast_to(_high_half(thr_c), (PACKED_ROWS, tq)))

    def count_ge_full(thr_c):
        thr8 = rows8(thr_c)
        return count(lambda s, kp: s >= thr8)

    n_all = (n_kb * KEY_BLOCK).astype(f32)
    carry = (jnp.zeros((1, tq), jnp.int32), jnp.zeros((1, tq), f32) + n_all)
    carry = lax.fori_loop(0, 16, functools.partial(bisect, count_ge=count_ge_high), carry)

    def unresolved(state):
        it, _, cnt = state
        return (it < 32) & (jnp.max(jnp.abs(cnt - float(k_sel))) > 0.0)

    def low_steps(state):
        it, tu, cnt = state
        tu, cnt = lax.fori_loop(it, it + EXIT_CHECK_STEPS,
                                functools.partial(bisect, count_ge=count_ge_full), (tu, cnt))
        return it + EXIT_CHECK_STEPS, tu, cnt

    _, tu, cnt_ge = lax.while_loop(unresolved, low_steps, (jnp.int32(16),) + carry)
    thr = _key_to_threshold(tu)

    j_s[...] = jnp.full((1, tq), seq, jnp.int32)

    @pl.when(jnp.max(cnt_ge) > float(k_sel))
    def _():
        thr8 = rows8(thr)
        need = float(k_sel) - count(lambda s, kp: s > thr8)
        nbits = (seq - 1).bit_length()
        def index_bit(it, p):
            cand = p + jnp.left_shift(jnp.int32(1), nbits - 1 - it)
            cand8 = rows8(cand)
            below = count(lambda s, kp: (s == thr8) & (kp < cand8))
            return jnp.where(below < need, cand, p)
        j_s[...] = lax.fori_loop(0, nbits, index_bit, jnp.zeros((1, tq), jnp.int32))

    m_s[...] = jnp.full(m_s.shape, -jnp.inf, f32)
    l_s[...] = jnp.zeros(l_s.shape, f32)
    acc_s[...] = jnp.zeros(acc_s.shape, f32)
    j_last = j_s[...]
    q_pos = i * tq + lax.broadcasted_iota(jnp.int32, (KEY_BLOCK, tq), 1)
    k_row = lax.broadcasted_iota(jnp.int32, (KEY_BLOCK, tq), 0)

    def attend_block(j, buf):
        k_pos = j * KEY_BLOCK + k_row
        s_idx = sc_s[j]
        sel = ((s_idx > thr) | ((s_idx == thr) & (k_pos <= j_last))) & (k_pos <= q_pos)
        mask_s[buf] = jnp.where(sel, 0.0, MASKED_LOGIT)
        kk = kr_ref[0, pl.ds(pl.multiple_of(j * KEY_BLOCK, KEY_BLOCK), KEY_BLOCK), :]
        vt = vtc_ref[0, j]
        for h in range(N_HEADS):
            s_s[buf, h] = _dot(kk, qw_s[h]) + mask_s[buf]
        for h in range(N_HEADS):
            m_old = m_s[h:h + 1, :]
            m_new = jnp.maximum(m_old, jnp.max(s_s[buf, h], axis=0, keepdims=True))
            alpha = jnp.exp2(m_old - m_new)
            p = jnp.exp2(s_s[buf, h] - m_new)
            l_s[h:h + 1, :] = alpha * l_s[h:h + 1, :] + jnp.sum(p, axis=0, keepdims=True)
            p_s[buf, h] = _mxu(p)
            m_s[h:h + 1, :] = m_new
            a_s[buf, h:h + 1, :] = alpha
        for h in range(N_HEADS):
            g = h // GROUP
            acc_s[h] = (a_s[buf, h:h + 1, :] * acc_s[h]
                        + _dot(vt[g * HEAD_DIM:(g + 1) * HEAD_DIM, :], p_s[buf, h]))

    def attend_pair(t, carry):
        attend_block(2 * t, 0)
        attend_block(2 * t + 1, 1)
        return carry
    lax.fori_loop(0, n_kb // 2, attend_pair, 0)

    @pl.when(n_kb % 2 == 1)
    def _():
        attend_block(n_kb - 1, 0)

    for h in range(N_HEADS):
        ot_s[h * HEAD_DIM:(h + 1) * HEAD_DIM, :] = acc_s[h] / l_s[h:h + 1, :]
    o_ref[0] = ot_s[...].T


def _attn_prompt(qt, iqt, iwt, k_rows, ikw_rows, vt_blocks, *, tq, cks):
    b, _, seq = qt.shape
    n_kb = seq // KEY_BLOCK
    k_sel = min(TOPK_MAX, seq // 4)
    colblk = lambda height: pl.BlockSpec((1, height, tq), lambda i, j: (i, 0, j))
    whole = lambda shape: pl.BlockSpec((1,) + shape, lambda i, j: (i,) + (0,) * len(shape))
    f32 = jnp.float32
    return pl.pallas_call(
        functools.partial(_attn_prompt_kernel, seq=seq, tq=tq, cks=cks, k_sel=k_sel),
        grid=(b, seq // tq),
        in_specs=[colblk(D_ATT), colblk(D_IQ), colblk(N_IDX_HEADS),
                  whole((seq, D_KV + LANES)), whole((seq, LANES)), whole((n_kb, D_KV, KEY_BLOCK)),
                  pl.BlockSpec((N_HEADS, LANES, tq), lambda i, j: (0, 0, 0))],
        out_specs=pl.BlockSpec((1, tq, D_ATT), lambda i, j: (i, j, 0)),
        out_shape=jax.ShapeDtypeStruct((b, seq, D_ATT), f32),
        scratch_shapes=[
            pltpu.VMEM((N_HEADS, D_KV + LANES, tq), MXU_DTYPE),
            pltpu.VMEM((N_IDX_HEADS, LANES, tq), MXU_DTYPE),
            pltpu.VMEM((n_kb, KEY_BLOCK, tq), f32),
            pltpu.VMEM((n_kb, KEY_BLOCK, tq), PACKED_DTYPE),
            pltpu.VMEM((2, KEY_BLOCK, tq), f32),
            pltpu.VMEM((N_HEADS, tq), f32),
            pltpu.VMEM((N_HEADS, tq), f32),
            pltpu.VMEM((2, N_HEADS, tq), f32),
            pltpu.VMEM((N_HEADS, HEAD_DIM, tq), f32),
            pltpu.VMEM((1, tq), jnp.int32),
            pltpu.VMEM((D_ATT, tq), f32),
            pltpu.VMEM((2, N_HEADS, KEY_BLOCK, tq), f32),
            pltpu.VMEM((2, N_HEADS, KEY_BLOCK, tq), MXU_DTYPE),
        ],
        compiler_params=pltpu.CompilerParams(
            dimension_semantics=("arbitrary", "arbitrary"), vmem_limit_bytes=VMEM_LIMIT),
        name="attn_prompt",
    )(qt, iqt, iwt, k_rows, ikw_rows, vt_blocks, _alibi_rows(tq))


def _select_sample_kernel(pt_ref, iq_ref, ikw_ref, ci_hbm, dist_ref, ibuf, sc_s, sem,
                          *, n_pages, t_new, kc, k_sel, group):
    step = pl.program_id(0)
    n_steps = pl.num_programs(0)
    slot = step % 2
    past = n_pages * PAGE_SIZE
    rows = group * t_new
    f32 = jnp.float32

    def page_copy(at_step, at_slot, g, p):
        phys = pt_ref[at_step * group + g, p]
        cols = pl.ds(pl.multiple_of(p * PAGE_SIZE, PAGE_SIZE), PAGE_SIZE)
        return pltpu.make_async_copy(ci_hbm.at[phys], ibuf.at[at_slot, g, :, cols], sem.at[at_slot])

    def start_pages(at_step, at_slot):
        def start_page(p, carry):
            for g in range(group):
                page_copy(at_step, at_slot, g, p).start()
            return carry
        lax.fori_loop(0, n_pages, start_page, 0)

    @pl.when(step == 0)
    def _():
        start_pages(0, 0)

    @pl.when(step + 1 < n_steps)
    def _():
        start_pages(step + 1, 1 - slot)

    def head_sum(r):
        acc = r[0:t_new]
        for h in range(1, N_HEADS):
            acc = acc + r[h * t_new:(h + 1) * t_new]
        return acc

    operands = []
    for g in range(group):
        ikw = ikw_ref[g]
        iq_rows, w_rows = [], []
        for h in range(N_IDX_HEADS):
            it = iq_ref[g, :, (h // 2) * LANES:(h // 2 + 1) * LANES]
            iq_rows.append((it if h % 2 == 0 else _swap_halves(it))[:, 0:IDX_DIM])
            w_rows.append(_lane_column(ikw, IDX_DIM + h) * (N_IDX_HEADS ** -0.5))
        pad = jnp.zeros((LANES - t_new, LANES), f32)
        ik_new = _mxu(jnp.concatenate([ikw, pad], axis=0)[:, 0:IDX_DIM])
        operands.append((_mxu(jnp.concatenate(iq_rows, axis=0)), jnp.concatenate(w_rows, axis=0), ik_new))

    def wait_page(p, carry):
        for g in range(group):
            page_copy(step, slot, g, p).wait()
        return carry
    lax.fori_loop(0, n_pages, wait_page, 0)

    t_id = lax.broadcasted_iota(jnp.int32, (t_new, LANES), 0)
    j_id = lax.broadcasted_iota(jnp.int32, (t_new, LANES), 1)
    for g, (iq_all, w_all, ik_new) in enumerate(operands):
        seq_rows = slice(g * t_new, (g + 1) * t_new)
        for c in range(past // kc):
            cols = slice(c * kc, (c + 1) * kc)
            sc_s[seq_rows, cols] = head_sum(
                jnp.maximum(_dot(iq_all, _mxu(ibuf[slot, g, :, cols])), 0.0) * w_all)
        s_new = head_sum(jnp.maximum(_dot_nt(iq_all, ik_new), 0.0) * w_all)
        sc_s[seq_rows, past:past + LANES] = jnp.where(j_id <= t_id, s_new, -jnp.inf)

    sc = sc_s[...]
    def count(mask):
        return jnp.sum(jnp.where(mask, 1.0, 0.0), axis=1, keepdims=True)

    def bisect(it, carry):
        tu, cnt_acc = carry
        shift = 30 - 2 * it
        for digit in (1, 2, 3):
            cand = tu | jnp.left_shift(jnp.int32(digit), shift)
            cnt = count(sc >= _key_to_threshold(cand))
            ok = cnt >= float(k_sel)
            best = jnp.where(ok, cand, tu if digit == 1 else best)
            cnt_acc = jnp.where(ok, cnt, cnt_acc)
        return best, cnt_acc

    tu, cnt_ge = lax.fori_loop(
        0, 16, bisect,
        (jnp.zeros((rows, 1), jnp.int32), jnp.full((rows, 1), float(past + LANES), f32)))
    thr = _key_to_threshold(tu)

    k_pos = lax.broadcasted_iota(jnp.int32, sc.shape, 1)
    q_pos = past + lax.broadcasted_iota(jnp.int32, sc.shape, 0) % t_new
    nbits = (past + LANES - 1).bit_length()

    def tie_limit():
        need = float(k_sel) - count(sc > thr)
        def index_bit(it, p):
            cand = p + jnp.left_shift(jnp.int32(1), nbits - 1 - it)
            below = count((sc == thr) & (k_pos < cand))
            return jnp.where(below < need, cand, p)
        return lax.fori_loop(0, nbits, index_bit, jnp.zeros((rows, 1), jnp.int32))

    j_last = lax.cond(jnp.max(cnt_ge) > float(k_sel), tie_limit,
                      lambda: jnp.full((rows, 1), past + LANES, jnp.int32))
    sel = ((sc > thr) | ((sc == thr) & (k_pos <= j_last))) & (k_pos <= q_pos)
    dist = jnp.where(sel, (k_pos - q_pos).astype(f32), MASKED_DIST)
    for g in range(group):
        dist_ref[g] = dist[g * t_new:(g + 1) * t_new]


def _select_sample(page_table, iq, ikw, cache_kit, *, kc, group):
    db, t_new, _ = iq.shape
    n_pages = page_table.shape[1]
    past = n_pages * PAGE_SIZE
    k_sel = min(TOPK_MAX, (past + t_new) // 4)
    blk = lambda width: pl.BlockSpec((group, t_new, width), lambda i, pt: (i, 0, 0))
    f32 = jnp.float32
    return pl.pallas_call(
        functools.partial(_select_sample_kernel, n_pages=n_pages, t_new=t_new, kc=kc, k_sel=k_sel,
                          group=group),
        grid_spec=pltpu.PrefetchScalarGridSpec(
            num_scalar_prefetch=1,
            grid=(db // group,),
            in_specs=[blk(D_IQ), blk(LANES), pl.BlockSpec(memory_space=pl.ANY)],
            out_specs=blk(past + LANES),
            scratch_shapes=[
                pltpu.VMEM((2, group, IDX_DIM, past), f32),
                pltpu.VMEM((group * t_new, past + LANES), f32),
                pltpu.SemaphoreType.DMA((2,)),
            ]),
        out_shape=jax.ShapeDtypeStruct((db, t_new, past + LANES), f32),
        compiler_params=pltpu.CompilerParams(
            dimension_semantics=("arbitrary",), vmem_limit_bytes=VMEM_LIMIT),
        name="select_sample",
    )(page_table, iq, ikw, cache_kit)


def _attn_sample_kernel(pt_ref, q_ref, k_ref, v_ref, dist_ref, ck_hbm, cv_hbm, o_ref,
                        kbuf, vbuf, s_s, sem,
                        *, n_pages, t_new, kc):
    b = pl.program_id(0)
    nb = pl.num_programs(0)
    slot = b % 2
    past = n_pages * PAGE_SIZE
    n_rows = N_HEADS * t_new
    f32 = jnp.float32

    def page_copies(seq_id, seq_slot, p):
        phys = pt_ref[seq_id, p]
        cols = pl.ds(pl.multiple_of(p * PAGE_SIZE, PAGE_SIZE), PAGE_SIZE)
        return (pltpu.make_async_copy(ck_hbm.at[phys], kbuf.at[seq_slot, :, cols], sem.at[seq_slot, 0]),
                pltpu.make_async_copy(cv_hbm.at[phys], vbuf.at[seq_slot, :, cols], sem.at[seq_slot, 1]))

    def start_pages(seq_id, seq_slot):
        def start_page(p, carry):
            for cp in page_copies(seq_id, seq_slot, p):
                cp.start()
            return carry
        lax.fori_loop(0, n_pages, start_page, 0)

    @pl.when(b == 0)
    def _():
        start_pages(0, 0)

    @pl.when(b + 1 < nb)
    def _():
        start_pages(b + 1, 1 - slot)

    lo = lax.broadcasted_iota(jnp.int32, (t_new, LANES), 1) < LANES // 2
    q_rows = []
    for h in range(N_HEADS):
        qt = q_ref[0, :, (h // 2) * LANES:(h // 2 + 1) * LANES]
        g, par = h // GROUP, h % 2
        if g != par:
            qt = _swap_halves(qt)
        q_rows.append(jnp.where(lo, qt, 0.0) if g == 0 else jnp.where(lo, 0.0, qt))
    q_all = _mxu(jnp.concatenate(q_rows, axis=0))
    slope = jnp.concatenate(
        [jnp.full((t_new, 1), _head_slope(h), f32) for h in range(N_HEADS)], axis=0)

    pad = jnp.zeros((LANES - t_new, LANES), f32)
    k_new = _mxu(jnp.concatenate([k_ref[0], pad], axis=0))
    v_new = _mxu(jnp.concatenate([v_ref[0], pad], axis=0))

    def wait_page(p, carry):
        for cp in page_copies(b, slot, p):
            cp.wait()
        return carry
    lax.fori_loop(0, n_pages, wait_page, 0)

    dist = dist_ref[0]

    def logits(s, d):
        return s + slope * jnp.concatenate([d] * N_HEADS, axis=0)

    m = jnp.full((n_rows, 1), -jnp.inf, f32)
    for c in range(past // kc):
        cols = slice(c * kc, (c + 1) * kc)
        s = logits(_dot(q_all, _mxu(kbuf[slot, :, cols])), dist[:, cols])
        s_s[:, cols] = s
        m = jnp.maximum(m, jnp.max(s, axis=1, keepdims=True))
    s = logits(_dot_nt(q_all, k_new), dist[:, past:past + LANES])
    s_s[:, past:past + LANES] = s
    m = jnp.maximum(m, jnp.max(s, axis=1, keepdims=True))

    l = jnp.zeros((n_rows, 1), f32)
    o = jnp.zeros((n_rows, LANES), f32)
    for c in range(past // kc):
        cols = slice(c * kc, (c + 1) * kc)
        p = jnp.exp(s_s[:, cols] - m)
        l = l + jnp.sum(p, axis=1, keepdims=True)
        o = o + _dot_nt(_mxu(p), _mxu(vbuf[slot, :, cols]))
    p = jnp.exp(s_s[:, past:past + LANES] - m)
    l = l + jnp.sum(p, axis=1, keepdims=True)
    o = (o + _dot(_mxu(p), v_new)) / l

    for j in range(N_HEADS // 2):
        o_ref[0, :, j * LANES:(j + 1) * LANES] = _pack_head_pair(
            o[(2 * j) * t_new:(2 * j + 1) * t_new], o[(2 * j + 1) * t_new:(2 * j + 2) * t_new],
            (2 * j) // GROUP)


def _attn_sample(page_table, q, k, v, dist, cache_kt, cache_vt, *, kc):
    db, t_new, _ = q.shape
    n_pages = page_table.shape[1]
    past = n_pages * PAGE_SIZE
    blk = lambda width: pl.BlockSpec((1, t_new, width), lambda i, pt: (i, 0, 0))
    hbm = pl.BlockSpec(memory_space=pl.ANY)
    f32 = jnp.float32
    return pl.pallas_call(
        functools.partial(_attn_sample_kernel, n_pages=n_pages, t_new=t_new, kc=kc),
        grid_spec=pltpu.PrefetchScalarGridSpec(
            num_scalar_prefetch=1,
            grid=(db,),
            in_specs=[blk(D_ATT), blk(D_KV), blk(D_KV), blk(past + LANES), hbm, hbm],
            out_specs=blk(D_ATT),
            scratch_shapes=[
                pltpu.VMEM((2, D_KV, past), f32),
                pltpu.VMEM((2, D_KV, past), f32),
                pltpu.VMEM((N_HEADS * t_new, past + LANES), f32),
                pltpu.SemaphoreType.DMA((2, 2)),
            ]),
        out_shape=jax.ShapeDtypeStruct((db, t_new, D_ATT), f32),
        compiler_params=pltpu.CompilerParams(
            dimension_semantics=("arbitrary",), vmem_limit_bytes=VMEM_LIMIT),
        name="attn_sample",
    )(page_table, q, k, v, dist, cache_kt, cache_vt)


def _prepare_weights(norm_gain, w_in, conv_w, w_up_conv, w_up_attn, w_out, final_gain):
    split = C_IKW + IDX_DIM + N_IDX_HEADS
    pad = jnp.zeros((D_MODEL, D_IN_PAD - D_IN_RAW), w_in.dtype)
    w_pad = _mxu(jnp.concatenate([w_in[:, :split], pad, w_in[:, split:]], axis=1))
    w_t = _mxu(jnp.concatenate(
        [w_in[:, C_Q:C_Q + D_ATT], w_in[:, C_IQ:C_IQ + D_IQ], w_in[:, C_K:C_K + 2 * D_KV],
         w_in[:, C_IKW:split]], axis=1).T)
    return (norm_gain, w_pad, w_t, conv_w), (_mxu(w_up_conv), _mxu(w_up_attn), _mxu(w_out), final_gain)


def _prompt_layer(x, proj_w, merge_w, *, tm_proj, tm_merge, tq, cks):
    b, t, _ = x.shape
    prev = jnp.zeros((b, CONV_WIDTH - 1, D_CONV), x.dtype)
    yc, az, gg, cs, qt, iqt, kt, vt, kit, iwt, k_rows, ikw_rows, vt_blocks = _project(
        x, prev, *proj_w, tm=tm_proj, tokens_on_lanes=True)
    attn = _attn_prompt(qt, iqt, iwt, k_rows, ikw_rows, vt_blocks, tq=tq, cks=cks)
    flat = lambda a: a.reshape(b * t, a.shape[-1])
    y = _merge(flat(x), flat(yc), flat(attn), flat(az), flat(gg), *merge_w, tm=tm_merge)
    heads = lambda a: a.reshape(b, N_KV_HEADS, HEAD_DIM, t).transpose(0, 3, 1, 2)[None]
    return y.reshape(b, t, D_MODEL), heads(kt), heads(vt), kit.transpose(0, 2, 1)[None], cs[None]


def _sample_layer(x, prev, page_table, cache_k, cache_v, cache_kidx, proj_w, merge_w,
                  *, tm_proj, tm_merge, kc, group):
    b, t, _ = x.shape
    n_phys = cache_k.shape[0]
    zeros = jnp.zeros((b, t, D_CONV), prev.dtype)
    one_back = zeros.at[:, 0].set(prev[:, 1])
    two_back = zeros.at[:, 0].set(prev[:, 0]).at[:, 1].set(prev[:, 1])
    tiles = (b * t) // tm_proj
    tiled = lambda a: a.reshape(tiles, tm_proj, a.shape[-1])
    outs = _project(tiled(x), jnp.stack([tiled(one_back), tiled(two_back)], axis=1), *proj_w,
                    tm=tm_proj, tokens_on_lanes=False, rows_per_seq=t)
    yc, az, gg, u, q, iq, ikw, k, v = [a.reshape(b, t, a.shape[-1]) for a in outs]
    cs = u[:, t - (CONV_WIDTH - 1):]
    cache_kt = cache_k.transpose(0, 2, 3, 1).reshape(n_phys, D_KV, PAGE_SIZE)
    cache_vt = cache_v.transpose(0, 2, 3, 1).reshape(n_phys, D_KV, PAGE_SIZE)
    cache_kit = cache_kidx.transpose(0, 2, 1)
    dist = _select_sample(page_table, iq, ikw, cache_kit, kc=kc, group=group)
    attn = _attn_sample(page_table, q, k, v, dist, cache_kt, cache_vt, kc=kc)
    flat = lambda a: a.reshape(b * t, a.shape[-1])
    y = _merge(flat(x), flat(yc), flat(attn), flat(az), flat(gg), *merge_w, tm=tm_merge)
    heads = lambda a: a.reshape(1, b, t, N_KV_HEADS, HEAD_DIM)
    return y.reshape(b, t, D_MODEL), heads(k), heads(v), ikw[None, :, :, :IDX_DIM], cs[None]


def kernel(x_prompt, x_sample, cache_k, cache_v, cache_kidx, state_conv, page_table, norm_gain, w_in,
           conv_w, w_up_conv, w_up_attn, w_out, final_gain):
    assert norm_gain.shape[0] == 1, "single-layer step"
    proj_w, merge_w = _prepare_weights(norm_gain[0], w_in[0], conv_w[0], w_up_conv[0], w_up_attn[0],
                                       w_out[0], final_gain)
    yp, kp, vp, ip, cp = _prompt_layer(x_prompt, proj_w, merge_w, tm_proj=512, tm_merge=512, tq=256, cks=512)
    ys, ks, vs, is_, cs = _sample_layer(x_sample, state_conv[0], page_table, cache_k[0], cache_v[0],
                                        cache_kidx[0], proj_w, merge_w, tm_proj=512, tm_merge=512, kc=1024,
                                        group=4)
    return (yp, ys, kp, vp, ip, cp, ks, vs, is_, cs)
```

```python
import functools

import jax
import jax.numpy as jnp
from jax import lax
from jax.experimental import pallas as pl
from jax.experimental.pallas import tpu as pltpu

D_MODEL = 1024
D_CONV = 512
CONV_WIDTH = 3
N_HEADS = 8
HEAD_DIM = 64
N_KV_HEADS = 2
GROUP = N_HEADS // N_KV_HEADS
D_ATT = N_HEADS * HEAD_DIM
D_KV = N_KV_HEADS * HEAD_DIM
N_IDX_HEADS = 8
IDX_DIM = 64
D_IQ = N_IDX_HEADS * IDX_DIM
TOPK_MAX = 256
PAGE_SIZE = 128
EPS = 1e-6

LANES = 128
SUBLANES = 8
VMEM_LIMIT = 48 * 1024 * 1024
MXU_DTYPE = jnp.bfloat16
GATE_DTYPE = jnp.bfloat16

C_CONV = 0
C_Q = 4 * D_CONV
C_K = C_Q + D_ATT
C_V = C_K + D_KV
C_AZ = C_V + D_KV
C_IQ = C_AZ + D_ATT
C_IKW = C_IQ + D_IQ
C_G = C_IKW + LANES
D_IN_RAW = C_IKW + IDX_DIM + N_IDX_HEADS + 2 * D_MODEL
D_IN_PAD = C_G + 2 * D_MODEL
R_Q = 0
R_IQ = R_Q + D_ATT
R_K = R_IQ + D_IQ
R_V = R_K + D_KV
R_IKW = R_V + D_KV
D_T = -(-(R_IKW + IDX_DIM + N_IDX_HEADS) // LANES) * LANES

MASKED_DIST = -1e33
MASKED_LOGIT = -1e30
LOG2E = 1.4426950408889634
N_LOG2E_PIECES = 4
POS_RADIX = 64
KEY_BLOCK = 2 * LANES
INT_MIN = -2 ** 31
KEY_NEG_INF = INT_MIN + 0x7FFFFF


def _dot(a, b):
    return jnp.dot(a, b, preferred_element_type=jnp.float32)


def _dot_nt(a, b):
    return lax.dot_general(a, b, (((1,), (1,)), ((), ())), preferred_element_type=jnp.float32)


def _mxu(x):
    return x.astype(MXU_DTYPE)


def _proj_kernel(x_ref, prev_ref, g_ref, w_ref, wt_ref, cw_ref, *refs, tm, tokens_on_lanes, rows_per_seq):
    yc_ref, az_ref, gg_ref, cs_ref = refs[:4]
    u_buf = refs[-1]
    t = pl.program_id(1)
    nt = pl.num_programs(1)
    x = x_ref[0]
    xn = x * lax.rsqrt(jnp.mean(x * x, axis=-1, keepdims=True) + EPS) * g_ref[...]
    xn = _mxu(xn)

    def proj(c0, width):
        return _dot(xn, w_ref[:, c0:c0 + width])

    if tokens_on_lanes:
        t_all = _dot_nt(wt_ref[...], xn)

    def proj_t(r0, height):
        return t_all[r0:r0 + height]

    u = proj(C_CONV + D_CONV, D_CONV) * proj(C_CONV + 2 * D_CONV, D_CONV)
    if rows_per_seq is None:
        @pl.when(t == 0)
        def _():
            u_buf[0:SUBLANES, :] = jnp.zeros((SUBLANES, D_CONV), jnp.float32)
            u_buf[SUBLANES - 2:SUBLANES, :] = prev_ref[0]

        @pl.when(t > 0)
        def _():
            u_buf[0:SUBLANES, :] = u_buf[tm:tm + SUBLANES, :]

        u_buf[SUBLANES:SUBLANES + tm, :] = u
        u_m1 = u_buf[SUBLANES - 1:SUBLANES - 1 + tm, :]
        u_m2 = u_buf[SUBLANES - 2:SUBLANES - 2 + tm, :]

        @pl.when(t == nt - 1)
        def _():
            cs_ref[0] = u_buf[tm + SUBLANES - 2:tm + SUBLANES, :]
    else:
        u_buf[0:SUBLANES, :] = jnp.zeros((SUBLANES, D_CONV), jnp.float32)
        u_buf[SUBLANES:SUBLANES + tm, :] = u
        row_in_seq = lax.broadcasted_iota(jnp.int32, (tm, D_CONV), 0) % rows_per_seq
        u_m1 = jnp.where(row_in_seq >= 1, u_buf[SUBLANES - 1:SUBLANES - 1 + tm, :], 0.0) + prev_ref[0, 0]
        u_m2 = jnp.where(row_in_seq >= 2, u_buf[SUBLANES - 2:SUBLANES - 2 + tm, :], 0.0) + prev_ref[0, 1]
        cs_ref[0] = u
    conv = cw_ref[0:1, :] * u_m2 + cw_ref[1:2, :] * u_m1 + cw_ref[2:3, :] * u
    yc_ref[0] = (proj(C_CONV, D_CONV) * conv * jax.nn.silu(proj(C_CONV + 3 * D_CONV, D_CONV))).astype(yc_ref.dtype)

    az_ref[0] = proj(C_AZ, D_ATT)
    gg_ref[0] = proj(C_G, 2 * D_MODEL).astype(gg_ref.dtype)
    if tokens_on_lanes:
        qt_ref, iqt_ref, kt_ref, vt_ref, kit_ref, iwt_ref, kr_ref, ikwr_ref, vtc_ref = refs[4:13]
        qt_ref[0] = _mxu(proj_t(R_Q, D_ATT) * (HEAD_DIM ** -0.5 * LOG2E))
        iqt_ref[0] = _mxu(proj_t(R_IQ, D_IQ) * (IDX_DIM ** -0.5))
        kt_ref[0] = proj_t(R_K, D_KV)
        vt = proj_t(R_V, D_KV)
        vt_ref[0] = vt
        for s in range(tm // KEY_BLOCK):
            vtc_ref[0, s] = _mxu(vt[:, s * KEY_BLOCK:(s + 1) * KEY_BLOCK])
        kit_ref[0] = proj_t(R_IKW, IDX_DIM)
        iwt_ref[0] = proj_t(R_IKW + IDX_DIM, N_IDX_HEADS)
        kr_ref[0, :, 0:D_KV] = _mxu(proj(C_K, D_KV))
        pos = t * tm + lax.broadcasted_iota(jnp.int32, (tm, LANES), 0)
        lane = lax.broadcasted_iota(jnp.int32, (tm, LANES), 1)
        feat = jnp.where(lane % 2 == 0, pos // POS_RADIX, pos % POS_RADIX)
        kr_ref[0, :, D_KV:D_KV + LANES] = _mxu(
            jnp.where(lane < 2 * N_LOG2E_PIECES, feat, 0).astype(jnp.float32))
        ikwr_ref[0] = _mxu(proj(C_IKW, LANES))
    else:
        q_ref, iq_ref, ikw_ref, k_ref, v_ref = refs[4:9]
        q_ref[0] = proj(C_Q, D_ATT) * (HEAD_DIM ** -0.5)
        iq_ref[0] = proj(C_IQ, D_IQ) * (IDX_DIM ** -0.5)
        ikw_ref[0] = proj(C_IKW, LANES)
        k_ref[0] = proj(C_K, D_KV)
        v_ref[0] = proj(C_V, D_KV)


def _project(x, prev, norm_gain, w_pad, w_t, conv_w, *, tm, tokens_on_lanes, rows_per_seq=None):
    b, t, _ = x.shape
    nt = t // tm
    row = lambda width: pl.BlockSpec((1, tm, width), lambda i, j: (i, j, 0))
    col = lambda height: pl.BlockSpec((1, height, tm), lambda i, j: (i, 0, j))
    full = lambda shape: pl.BlockSpec(shape, lambda i, j: (0,) * len(shape), pipeline_mode=pl.Buffered(1))
    f32 = jnp.float32
    sds = jax.ShapeDtypeStruct
    if rows_per_seq is None:
        state_in = state_out = pl.BlockSpec((1, CONV_WIDTH - 1, D_CONV), lambda i, j: (i, 0, 0))
        state_shape = sds((b, CONV_WIDTH - 1, D_CONV), f32)
    else:
        assert tm == t and t % rows_per_seq == 0 and rows_per_seq >= CONV_WIDTH - 1
        state_in = pl.BlockSpec((1, CONV_WIDTH - 1, t, D_CONV), lambda i, j: (i, 0, 0, 0))
        state_out, state_shape = row(D_CONV), sds((b, t, D_CONV), f32)
    out_shapes = [sds((b, t, D_CONV), MXU_DTYPE), sds((b, t, D_ATT), f32), sds((b, t, 2 * D_MODEL), GATE_DTYPE),
                  state_shape]
    out_specs = [row(D_CONV), row(D_ATT), row(2 * D_MODEL), state_out]
    if tokens_on_lanes:
        out_shapes += [sds((b, D_ATT, t), MXU_DTYPE), sds((b, D_IQ, t), MXU_DTYPE),
                       sds((b, D_KV, t), f32), sds((b, D_KV, t), f32), sds((b, IDX_DIM, t), f32),
                       sds((b, N_IDX_HEADS, t), f32),
                       sds((b, t, D_KV + LANES), MXU_DTYPE), sds((b, t, LANES), MXU_DTYPE),
                       sds((b, t // KEY_BLOCK, D_KV, KEY_BLOCK), MXU_DTYPE)]
        out_specs += [col(D_ATT), col(D_IQ), col(D_KV), col(D_KV), col(IDX_DIM), col(N_IDX_HEADS),
                      row(D_KV + LANES), row(LANES),
                      pl.BlockSpec((1, tm // KEY_BLOCK, D_KV, KEY_BLOCK), lambda i, j: (i, j, 0, 0))]
    else:
        out_shapes += [sds((b, t, D_ATT), f32), sds((b, t, D_IQ), f32), sds((b, t, LANES), f32),
                       sds((b, t, D_KV), f32), sds((b, t, D_KV), f32)]
        out_specs += [row(D_ATT), row(D_IQ), row(LANES), row(D_KV), row(D_KV)]
    return pl.pallas_call(
        functools.partial(_proj_kernel, tm=tm, tokens_on_lanes=tokens_on_lanes, rows_per_seq=rows_per_seq),
        grid=(b, nt),
        in_specs=[row(D_MODEL), state_in, full((1, D_MODEL)), full((D_MODEL, D_IN_PAD)),
                  full((D_T, D_MODEL)), full((CONV_WIDTH, D_CONV))],
        out_specs=out_specs,
        out_shape=out_shapes,
        scratch_shapes=[pltpu.VMEM((tm + SUBLANES, D_CONV), f32)],
        compiler_params=pltpu.CompilerParams(
            dimension_semantics=("arbitrary", "arbitrary"), vmem_limit_bytes=VMEM_LIMIT),
        name="proj",
    )(x, prev, norm_gain.reshape(1, D_MODEL), w_pad, w_t, conv_w)


def _merge_kernel(x_ref, yc_ref, at_ref, az_ref, gg_ref, wuc_ref, wua_ref, wo_ref, fg_ref, y_ref):
    ya = _mxu(at_ref[...] * jax.nn.silu(az_ref[...]))
    gc = gg_ref[:, 0:D_MODEL].astype(jnp.float32)
    ga = gg_ref[:, D_MODEL:2 * D_MODEL].astype(jnp.float32)
    m = (jax.nn.sigmoid(gc) * _dot(yc_ref[...], wuc_ref[...])
         + jax.nn.sigmoid(ga) * _dot(ya, wua_ref[...]))
    xo = x_ref[...] + _dot(_mxu(m), wo_ref[...])
    y_ref[...] = xo * lax.rsqrt(jnp.mean(xo * xo, axis=-1, keepdims=True) + EPS) * fg_ref[...]


def _merge(x, yc, attn, az, gg, wuc, wua, wo, final_gain, *, tm):
    n = x.shape[0]
    row = lambda width: pl.BlockSpec((tm, width), lambda i: (i, 0))
    full = lambda shape: pl.BlockSpec(shape, lambda i: (0,) * len(shape), pipeline_mode=pl.Buffered(1))
    return pl.pallas_call(
        _merge_kernel,
        grid=(n // tm,),
        in_specs=[row(D_MODEL), row(D_CONV), row(D_ATT), row(D_ATT), row(2 * D_MODEL),
                  full((D_CONV, D_MODEL)), full((D_ATT, D_MODEL)), full((D_MODEL, D_MODEL)),
                  full((1, D_MODEL))],
        out_specs=row(D_MODEL),
        out_shape=jax.ShapeDtypeStruct((n, D_MODEL), jnp.float32),
        compiler_params=pltpu.CompilerParams(
            dimension_semantics=("arbitrary",), vmem_limit_bytes=VMEM_LIMIT),
        name="merge",
    )(x, yc, attn, az, gg, wuc, wua, wo, final_gain.reshape(1, D_MODEL))


def _key_to_threshold(tu):
    key = tu ^ jnp.int32(INT_MIN)
    bits = key ^ ((key >> 31) & jnp.int32(0x7FFFFFFF))
    thr = lax.bitcast_convert_type(bits, jnp.float32)
    return jnp.where(key < jnp.int32(KEY_NEG_INF), -jnp.inf, thr)


PACKED_DTYPE = jnp.bfloat16
PACKED_ROWS = 2 * SUBLANES


def _high_half(x):
    bits = lax.bitcast_convert_type(x, jnp.int32) & jnp.int32(-65536)
    return lax.bitcast_convert_type(bits, jnp.float32).astype(PACKED_DTYPE)


def _lane_column(x, lane):
    ids = lax.broadcasted_iota(jnp.int32, x.shape, 1)
    return jnp.sum(jnp.where(ids == lane, x, 0.0), axis=1, keepdims=True)


def _swap_halves(x):
    return pltpu.roll(x, LANES // 2, axis=1)


def _head_slope(h):
    return 2.0 ** (-8.0 * (h + 1) / N_HEADS)


def _pack_head_pair(a0, a1, kv_head):
    lo = lax.broadcasted_iota(jnp.int32, a0.shape, 1) < LANES // 2
    if kv_head == 0:
        return jnp.where(lo, a0, _swap_halves(a1))
    return jnp.where(lo, _swap_halves(a0), a1)


N_COUNT_ACC = 4
EXIT_CHECK_STEPS = 4

def _alibi_rows(tq):
    pieces, rest = [], jnp.float32(LOG2E)
    for _ in range(N_LOG2E_PIECES):
        piece = rest.astype(MXU_DTYPE).astype(jnp.float32)
        pieces.append(piece)
        rest = rest - piece
    rows = jnp.zeros((N_HEADS, LANES), jnp.float32)
    for h in range(N_HEADS):
        for j, piece in enumerate(pieces):
            rows = rows.at[h, 2 * j].set(_head_slope(h) * POS_RADIX * piece)
            rows = rows.at[h, 2 * j + 1].set(_head_slope(h) * piece)
    return _mxu(jnp.broadcast_to(rows[:, :, None], (N_HEADS, LANES, tq)))


def _attn_prompt_kernel(qt_ref, iqt_ref, iwt_ref, kr_ref, ikwr_ref, vtc_ref, aw_ref, o_ref,
                        qw_s, iqw_s, sc_s, sh_s, mask_s, m_s, l_s, a_s, acc_s, j_s, ot_s, s_s, p_s,
                        *, seq, tq, cks, k_sel):
    i = pl.program_id(1)
    n_kb = (i + 1) * (tq // KEY_BLOCK)
    n_cks = ((i + 1) * tq + cks - 1) // cks
    f32 = jnp.float32

    zeros = jnp.zeros((HEAD_DIM, tq), MXU_DTYPE)
    for h in range(N_HEADS):
        g = h // GROUP
        qw_s[h, g * HEAD_DIM:(g + 1) * HEAD_DIM, :] = qt_ref[0, h * HEAD_DIM:(h + 1) * HEAD_DIM, :]
        qw_s[h, (1 - g) * HEAD_DIM:(2 - g) * HEAD_DIM, :] = zeros
        qw_s[h, D_KV:D_KV + LANES, :] = aw_ref[h]
        iqw_s[h, 0:IDX_DIM, :] = iqt_ref[0, h * IDX_DIM:(h + 1) * IDX_DIM, :]
        iqw_s[h, IDX_DIM:2 * IDX_DIM, :] = zeros

    w_rows = iwt_ref[0] * (N_IDX_HEADS ** -0.5)
    q_pos_c = i * tq + lax.broadcasted_iota(jnp.int32, (cks, tq), 1)
    k_row_c = lax.broadcasted_iota(jnp.int32, (cks, tq), 0)

    def score_chunk(c, carry):
        kk = ikwr_ref[0, pl.ds(pl.multiple_of(c * cks, cks), cks), :]
        acc = jnp.zeros((cks, tq), f32)
        for h in range(N_IDX_HEADS):
            acc = acc + jnp.maximum(_dot(kk, iqw_s[h]), 0.0) * w_rows[h:h + 1, :]
        acc = jnp.where(c * cks + k_row_c <= q_pos_c, acc, -jnp.inf)
        for t in range(cks // KEY_BLOCK):
            blk = acc[t * KEY_BLOCK:(t + 1) * KEY_BLOCK]
            sc_s[c * (cks // KEY_BLOCK) + t] = blk
            sh_s[c * (cks // KEY_BLOCK) + t] = _high_half(blk)
        return carry
    lax.fori_loop(0, n_cks, score_chunk, 0)

    sub_id = lax.broadcasted_iota(jnp.int32, (SUBLANES, tq), 0)

    def count(pred):
        def body(j, parts):
            parts = list(parts)
            for r in range(KEY_BLOCK // SUBLANES):
                rows = sc_s[j, r * SUBLANES:(r + 1) * SUBLANES, :]
                hit = pred(rows, j * KEY_BLOCK + r * SUBLANES + sub_id)
                parts[r % N_COUNT_ACC] = parts[r % N_COUNT_ACC] + jnp.where(hit, 1.0, 0.0)
            return tuple(parts)
        parts = lax.fori_loop(0, n_kb, body, (jnp.zeros((SUBLANES, tq), f32),) * N_COUNT_ACC)
        return jnp.sum(functools.reduce(lambda a, b: a + b, parts), axis=0, keepdims=True)

    def rows8(row):
        return jnp.broadcast_to(row, (SUBLANES, tq))

    def count_high(thr_high):
        one = jnp.ones((PACKED_ROWS, tq), PACKED_DTYPE)
        zero = jnp.zeros((PACKED_ROWS, tq), PACKED_DTYPE)
        def body(j, parts):
            parts = list(parts)
            for r in range(KEY_BLOCK // PACKED_ROWS):
                rows = sh_s[j, r * PACKED_ROWS:(r + 1) * PACKED_ROWS, :]
                parts[r % N_COUNT_ACC] = parts[r % N_COUNT_ACC] + jnp.where(rows >= thr_high, one, zero)
            return tuple(parts)
        parts = lax.fori_loop(0, n_kb, body, (zero,) * N_COUNT_ACC)
        total = functools.reduce(lambda a, b: a + b, [p.astype(f32) for p in parts])
        return jnp.sum(total, axis=0, keepdims=True)

    def bisect(it, carry, count_ge):
        tu, cnt_acc = carry
        cand = tu | jnp.left_shift(jnp.int32(1), 31 - it)
        cnt = count_ge(_key_to_threshold(cand))
        ok = cnt >= float(k_sel)
        return jnp.where(ok, cand, tu), jnp.where(ok, cnt, cnt_acc)

    def count_ge_high(thr_c):
        return count_high(jnp.broadcast_to(_high_half(thr_c), (PACKED_ROWS, tq)))

    def count_ge_full(thr_c):
        thr8 = rows8(thr_c)
        return count(lambda s, kp: s >= thr8)

    n_all = (n_kb * KEY_BLOCK).astype(f32)
    carry = (jnp.zeros((1, tq), jnp.int32), jnp.zeros((1, tq), f32) + n_all)
    carry = lax.fori_loop(0, 16, functools.partial(bisect, count_ge=count_ge_high), carry)

    def unresolved(state):
        it, _, cnt = state
        return (it < 32) & (jnp.max(jnp.abs(cnt - float(k_sel))) > 0.0)

    def low_steps(state):
        it, tu, cnt = state
        tu, cnt = lax.fori_loop(it, it + EXIT_CHECK_STEPS,
                                functools.partial(bisect, count_ge=count_ge_full), (tu, cnt))
        return it + EXIT_CHECK_STEPS, tu, cnt

    _, tu, cnt_ge = lax.while_loop(unresolved, low_steps, (jnp.int32(16),) + carry)
    thr = _key_to_threshold(tu)

    j_s[...] = jnp.full((1, tq), seq, jnp.int32)

    @pl.when(jnp.max(cnt_ge) > float(k_sel))
    def _():
        thr8 = rows8(thr)
        need = float(k_sel) - count(lambda s, kp: s > thr8)
        nbits = (seq - 1).bit_length()
        def index_bit(it, p):
            cand = p + jnp.left_shift(jnp.int32(1), nbits - 1 - it)
            cand8 = rows8(cand)
            below = count(lambda s, kp: (s == thr8) & (kp < cand8))
            return jnp.where(below < need, cand, p)
        j_s[...] = lax.fori_loop(0, nbits, index_bit, jnp.zeros((1, tq), jnp.int32))

    m_s[...] = jnp.full(m_s.shape, -jnp.inf, f32)
    l_s[...] = jnp.zeros(l_s.shape, f32)
    acc_s[...] = jnp.zeros(acc_s.shape, f32)
    j_last = j_s[...]
    q_pos = i * tq + lax.broadcasted_iota(jnp.int32, (KEY_BLOCK, tq), 1)
    k_row = lax.broadcasted_iota(jnp.int32, (KEY_BLOCK, tq), 0)

    def attend_block(j, buf):
        k_pos = j * KEY_BLOCK + k_row
        s_idx = sc_s[j]
        sel = ((s_idx > thr) | ((s_idx == thr) & (k_pos <= j_last))) & (k_pos <= q_pos)
        mask_s[buf] = jnp.where(sel, 0.0, MASKED_LOGIT)
        kk = kr_ref[0, pl.ds(pl.multiple_of(j * KEY_BLOCK, KEY_BLOCK), KEY_BLOCK), :]
        vt = vtc_ref[0, j]
        for h in range(N_HEADS):
            s_s[buf, h] = _dot(kk, qw_s[h]) + mask_s[buf]
        for h in range(N_HEADS):
            m_old = m_s[h:h + 1, :]
            m_new = jnp.maximum(m_old, jnp.max(s_s[buf, h], axis=0, keepdims=True))
            alpha = jnp.exp2(m_old - m_new)
            p = jnp.exp2(s_s[buf, h] - m_new)
            l_s[h:h + 1, :] = alpha * l_s[h:h + 1, :] + jnp.sum(p, axis=0, keepdims=True)
            p_s[buf, h] = _mxu(p)
            m_s[h:h + 1, :] = m_new
            a_s[buf, h:h + 1, :] = alpha
        for h in range(N_HEADS):
            g = h // GROUP
            acc_s[h] = (a_s[buf, h:h + 1, :] * acc_s[h]
                        + _dot(vt[g * HEAD_DIM:(g + 1) * HEAD_DIM, :], p_s[buf, h]))

    def attend_pair(t, carry):
        attend_block(2 * t, 0)
        attend_block(2 * t + 1, 1)
        return carry
    lax.fori_loop(0, n_kb // 2, attend_pair, 0)

    @pl.when(n_kb % 2 == 1)
    def _():
        attend_block(n_kb - 1, 0)

    for h in range(N_HEADS):
        ot_s[h * HEAD_DIM:(h + 1) * HEAD_DIM, :] = acc_s[h] / l_s[h:h + 1, :]
    o_ref[0] = ot_s[...].T


def _attn_prompt(qt, iqt, iwt, k_rows, ikw_rows, vt_blocks, *, tq, cks):
    b, _, seq = qt.shape
    n_kb = seq // KEY_BLOCK
    k_sel = min(TOPK_MAX, seq // 4)
    assert n_kb * (KEY_BLOCK // PACKED_ROWS) // N_COUNT_ACC <= 256
    colblk = lambda height: pl.BlockSpec((1, height, tq), lambda i, j: (i, 0, j))
    whole = lambda shape: pl.BlockSpec((1,) + shape, lambda i, j: (i,) + (0,) * len(shape))
    f32 = jnp.float32
    return pl.pallas_call(
        functools.partial(_attn_prompt_kernel, seq=seq, tq=tq, cks=cks, k_sel=k_sel),
        grid=(b, seq // tq),
        in_specs=[colblk(D_ATT), colblk(D_IQ), colblk(N_IDX_HEADS),
                  whole((seq, D_KV + LANES)), whole((seq, LANES)), whole((n_kb, D_KV, KEY_BLOCK)),
                  pl.BlockSpec((N_HEADS, LANES, tq), lambda i, j: (0, 0, 0))],
        out_specs=pl.BlockSpec((1, tq, D_ATT), lambda i, j: (i, j, 0)),
        out_shape=jax.ShapeDtypeStruct((b, seq, D_ATT), f32),
        scratch_shapes=[
            pltpu.VMEM((N_HEADS, D_KV + LANES, tq), MXU_DTYPE),
            pltpu.VMEM((N_IDX_HEADS, LANES, tq), MXU_DTYPE),
            pltpu.VMEM((n_kb, KEY_BLOCK, tq), f32),
            pltpu.VMEM((n_kb, KEY_BLOCK, tq), PACKED_DTYPE),
            pltpu.VMEM((2, KEY_BLOCK, tq), f32),
            pltpu.VMEM((N_HEADS, tq), f32),
            pltpu.VMEM((N_HEADS, tq), f32),
            pltpu.VMEM((2, N_HEADS, tq), f32),
            pltpu.VMEM((N_HEADS, HEAD_DIM, tq), f32),
            pltpu.VMEM((1, tq), jnp.int32),
            pltpu.VMEM((D_ATT, tq), f32),
            pltpu.VMEM((2, N_HEADS, KEY_BLOCK, tq), f32),
            pltpu.VMEM((2, N_HEADS, KEY_BLOCK, tq), MXU_DTYPE),
        ],
        compiler_params=pltpu.CompilerParams(
            dimension_semantics=("arbitrary", "arbitrary"), vmem_limit_bytes=VMEM_LIMIT),
        name="attn_prompt",
    )(qt, iqt, iwt, k_rows, ikw_rows, vt_blocks, _alibi_rows(tq))


def _select_sample_kernel(pt_ref, iq_ref, ikw_ref, ci_hbm, dist_ref, ibuf, sc_s, sem,
                          *, n_pages, t_new, kc, k_sel, group):
    step = pl.program_id(0)
    n_steps = pl.num_programs(0)
    slot = step % 2
    past = n_pages * PAGE_SIZE
    rows = group * t_new
    f32 = jnp.float32

    def page_copy(at_step, at_slot, g, p):
        phys = pt_ref[at_step * group + g, p]
        cols = pl.ds(pl.multiple_of(p * PAGE_SIZE, PAGE_SIZE), PAGE_SIZE)
        return pltpu.make_async_copy(ci_hbm.at[phys], ibuf.at[at_slot, g, :, cols], sem.at[at_slot])

    def start_pages(at_step, at_slot):
        def start_page(p, carry):
            for g in range(group):
                page_copy(at_step, at_slot, g, p).start()
            return carry
        lax.fori_loop(0, n_pages, start_page, 0)

    @pl.when(step == 0)
    def _():
        start_pages(0, 0)

    @pl.when(step + 1 < n_steps)
    def _():
        start_pages(step + 1, 1 - slot)

    def head_sum(r):
        acc = r[0:t_new]
        for h in range(1, N_HEADS):
            acc = acc + r[h * t_new:(h + 1) * t_new]
        return acc

    operands = []
    for g in range(group):
        ikw = ikw_ref[g]
        iq_rows, w_rows = [], []
        for h in range(N_IDX_HEADS):
            it = iq_ref[g, :, (h // 2) * LANES:(h // 2 + 1) * LANES]
            iq_rows.append((it if h % 2 == 0 else _swap_halves(it))[:, 0:IDX_DIM])
            w_rows.append(_lane_column(ikw, IDX_DIM + h) * (N_IDX_HEADS ** -0.5))
        pad = jnp.zeros((LANES - t_new, LANES), f32)
        ik_new = _mxu(jnp.concatenate([ikw, pad], axis=0)[:, 0:IDX_DIM])
        operands.append((_mxu(jnp.concatenate(iq_rows, axis=0)), jnp.concatenate(w_rows, axis=0), ik_new))

    def wait_page(p, carry):
        for g in range(group):
            page_copy(step, slot, g, p).wait()
        return carry
    lax.fori_loop(0, n_pages, wait_page, 0)

    t_id = lax.broadcasted_iota(jnp.int32, (t_new, LANES), 0)
    j_id = lax.broadcasted_iota(jnp.int32, (t_new, LANES), 1)
    for g, (iq_all, w_all, ik_new) in enumerate(operands):
        seq_rows = slice(g * t_new, (g + 1) * t_new)
        for c in range(past // kc):
            cols = slice(c * kc, (c + 1) * kc)
            sc_s[seq_rows, cols] = head_sum(
                jnp.maximum(_dot(iq_all, _mxu(ibuf[slot, g, :, cols])), 0.0) * w_all)
        s_new = head_sum(jnp.maximum(_dot_nt(iq_all, ik_new), 0.0) * w_all)
        sc_s[seq_rows, past:past + LANES] = jnp.where(j_id <= t_id, s_new, -jnp.inf)

    sc = sc_s[...]
    def count(mask):
        return jnp.sum(jnp.where(mask, 1.0, 0.0), axis=1, keepdims=True)

    def bisect(it, carry):
        tu, cnt_acc = carry
        shift = 30 - 2 * it
        for digit in (1, 2, 3):
            cand = tu | jnp.left_shift(jnp.int32(digit), shift)
            cnt = count(sc >= _key_to_threshold(cand))
            ok = cnt >= float(k_sel)
            best = jnp.where(ok, cand, tu if digit == 1 else best)
            cnt_acc = jnp.where(ok, cnt, cnt_acc)
        return best, cnt_acc

    tu, cnt_ge = lax.fori_loop(
        0, 16, bisect,
        (jnp.zeros((rows, 1), jnp.int32), jnp.full((rows, 1), float(past + LANES), f32)))
    thr = _key_to_threshold(tu)

    k_pos = lax.broadcasted_iota(jnp.int32, sc.shape, 1)
    q_pos = past + lax.broadcasted_iota(jnp.int32, sc.shape, 0) % t_new
    nbits = (past + LANES - 1).bit_length()

    def tie_limit():
        need = float(k_sel) - count(sc > thr)
        def index_bit(it, p):
            cand = p + jnp.left_shift(jnp.int32(1), nbits - 1 - it)
            below = count((sc == thr) & (k_pos < cand))
            return jnp.where(below < need, cand, p)
        return lax.fori_loop(0, nbits, index_bit, jnp.zeros((rows, 1), jnp.int32))

    j_last = lax.cond(jnp.max(cnt_ge) > float(k_sel), tie_limit,
                      lambda: jnp.full((rows, 1), past + LANES, jnp.int32))
    sel = ((sc > thr) | ((sc == thr) & (k_pos <= j_last))) & (k_pos <= q_pos)
    dist = jnp.where(sel, (k_pos - q_pos).astype(f32), MASKED_DIST)
    for g in range(group):
        dist_ref[g] = dist[g * t_new:(g + 1) * t_new]


def _select_sample(page_table, iq, ikw, cache_kit, *, kc, group):
    db, t_new, _ = iq.shape
    n_pages = page_table.shape[1]
    past = n_pages * PAGE_SIZE
    k_sel = min(TOPK_MAX, (past + t_new) // 4)
    blk = lambda width: pl.BlockSpec((group, t_new, width), lambda i, pt: (i, 0, 0))
    f32 = jnp.float32
    return pl.pallas_call(
        functools.partial(_select_sample_kernel, n_pages=n_pages, t_new=t_new, kc=kc, k_sel=k_sel,
                          group=group),
        grid_spec=pltpu.PrefetchScalarGridSpec(
            num_scalar_prefetch=1,
            grid=(db // group,),
            in_specs=[blk(D_IQ), blk(LANES), pl.BlockSpec(memory_space=pl.ANY)],
            out_specs=blk(past + LANES),
            scratch_shapes=[
                pltpu.VMEM((2, group, IDX_DIM, past), f32),
                pltpu.VMEM((group * t_new, past + LANES), f32),
                pltpu.SemaphoreType.DMA((2,)),
            ]),
        out_shape=jax.ShapeDtypeStruct((db, t_new, past + LANES), f32),
        compiler_params=pltpu.CompilerParams(
            dimension_semantics=("arbitrary",), vmem_limit_bytes=VMEM_LIMIT),
        name="select_sample",
    )(page_table, iq, ikw, cache_kit)


def _attn_sample_kernel(pt_ref, q_ref, k_ref, v_ref, dist_ref, ck_hbm, cv_hbm, o_ref,
                        kbuf, vbuf, s_s, sem,
                        *, n_pages, t_new, kc):
    b = pl.program_id(0)
    nb = pl.num_programs(0)
    slot = b % 2
    past = n_pages * PAGE_SIZE
    n_rows = N_HEADS * t_new
    f32 = jnp.float32

    def page_copies(seq_id, seq_slot, p):
        phys = pt_ref[seq_id, p]
        cols = pl.ds(pl.multiple_of(p * PAGE_SIZE, PAGE_SIZE), PAGE_SIZE)
        return (pltpu.make_async_copy(ck_hbm.at[phys], kbuf.at[seq_slot, :, cols], sem.at[seq_slot, 0]),
                pltpu.make_async_copy(cv_hbm.at[phys], vbuf.at[seq_slot, :, cols], sem.at[seq_slot, 1]))

    def start_pages(seq_id, seq_slot):
        def start_page(p, carry):
            for cp in page_copies(seq_id, seq_slot, p):
                cp.start()
            return carry
        lax.fori_loop(0, n_pages, start_page, 0)

    @pl.when(b == 0)
    def _():
        start_pages(0, 0)

    @pl.when(b + 1 < nb)
    def _():
        start_pages(b + 1, 1 - slot)

    lo = lax.broadcasted_iota(jnp.int32, (t_new, LANES), 1) < LANES // 2
    q_rows = []
    for h in range(N_HEADS):
        qt = q_ref[0, :, (h // 2) * LANES:(h // 2 + 1) * LANES]
        g, par = h // GROUP, h % 2
        if g != par:
            qt = _swap_halves(qt)
        q_rows.append(jnp.where(lo, qt, 0.0) if g == 0 else jnp.where(lo, 0.0, qt))
    q_all = _mxu(jnp.concatenate(q_rows, axis=0))
    slope = jnp.concatenate(
        [jnp.full((t_new, 1), _head_slope(h), f32) for h in range(N_HEADS)], axis=0)

    pad = jnp.zeros((LANES - t_new, LANES), f32)
    k_new = _mxu(jnp.concatenate([k_ref[0], pad], axis=0))
    v_new = _mxu(jnp.concatenate([v_ref[0], pad], axis=0))

    def wait_page(p, carry):
        for cp in page_copies(b, slot, p):
            cp.wait()
        return carry
    lax.fori_loop(0, n_pages, wait_page, 0)

    dist = dist_ref[0]

    def logits(s, d):
        return s + slope * jnp.concatenate([d] * N_HEADS, axis=0)

    m = jnp.full((n_rows, 1), -jnp.inf, f32)
    for c in range(past // kc):
        cols = slice(c * kc, (c + 1) * kc)
        s = logits(_dot(q_all, _mxu(kbuf[slot, :, cols])), dist[:, cols])
        s_s[:, cols] = s
        m = jnp.maximum(m, jnp.max(s, axis=1, keepdims=True))
    s = logits(_dot_nt(q_all, k_new), dist[:, past:past + LANES])
    s_s[:, past:past + LANES] = s
    m = jnp.maximum(m, jnp.max(s, axis=1, keepdims=True))

    l = jnp.zeros((n_rows, 1), f32)
    o = jnp.zeros((n_rows, LANES), f32)
    for c in range(past // kc):
        cols = slice(c * kc, (c + 1) * kc)
        p = jnp.exp(s_s[:, cols] - m)
        l = l + jnp.sum(p, axis=1, keepdims=True)
        o = o + _dot_nt(_mxu(p), _mxu(vbuf[slot, :, cols]))
    p = jnp.exp(s_s[:, past:past + LANES] - m)
    l = l + jnp.sum(p, axis=1, keepdims=True)
    o = (o + _dot(_mxu(p), v_new)) / l

    for j in range(N_HEADS // 2):
        o_ref[0, :, j * LANES:(j + 1) * LANES] = _pack_head_pair(
            o[(2 * j) * t_new:(2 * j + 1) * t_new], o[(2 * j + 1) * t_new:(2 * j + 2) * t_new],
            (2 * j) // GROUP)


def _attn_sample(page_table, q, k, v, dist, cache_kt, cache_vt, *, kc):
    db, t_new, _ = q.shape
    n_pages = page_table.shape[1]
    past = n_pages * PAGE_SIZE
    blk = lambda width: pl.BlockSpec((1, t_new, width), lambda i, pt: (i, 0, 0))
    hbm = pl.BlockSpec(memory_space=pl.ANY)
    f32 = jnp.float32
    return pl.pallas_call(
        functools.partial(_attn_sample_kernel, n_pages=n_pages, t_new=t_new, kc=kc),
        grid_spec=pltpu.PrefetchScalarGridSpec(
            num_scalar_prefetch=1,
            grid=(db,),
            in_specs=[blk(D_ATT), blk(D_KV), blk(D_KV), blk(past + LANES), hbm, hbm],
            out_specs=blk(D_ATT),
            scratch_shapes=[
                pltpu.VMEM((2, D_KV, past), f32),
                pltpu.VMEM((2, D_KV, past), f32),
                pltpu.VMEM((N_HEADS * t_new, past + LANES), f32),
                pltpu.SemaphoreType.DMA((2, 2)),
            ]),
        out_shape=jax.ShapeDtypeStruct((db, t_new, D_ATT), f32),
        compiler_params=pltpu.CompilerParams(
            dimension_semantics=("arbitrary",), vmem_limit_bytes=VMEM_LIMIT),
        name="attn_sample",
    )(page_table, q, k, v, dist, cache_kt, cache_vt)


def _prepare_weights(norm_gain, w_in, conv_w, w_up_conv, w_up_attn, w_out, final_gain):
    split = C_IKW + IDX_DIM + N_IDX_HEADS
    pad = jnp.zeros((D_MODEL, D_IN_PAD - D_IN_RAW), w_in.dtype)
    w_pad = _mxu(jnp.concatenate([w_in[:, :split], pad, w_in[:, split:]], axis=1))
    used_t = R_IKW + IDX_DIM + N_IDX_HEADS
    w_t = _mxu(jnp.concatenate(
        [w_in[:, C_Q:C_Q + D_ATT], w_in[:, C_IQ:C_IQ + D_IQ], w_in[:, C_K:C_K + 2 * D_KV],
         w_in[:, C_IKW:split], jnp.zeros((D_MODEL, D_T - used_t), w_in.dtype)], axis=1).T)
    return (norm_gain, w_pad, w_t, conv_w), (_mxu(w_up_conv), _mxu(w_up_attn), _mxu(w_out), final_gain)


def _prompt_layer(x, proj_w, merge_w, *, tm_proj, tm_merge, tq, cks):
    b, t, _ = x.shape
    prev = jnp.zeros((b, CONV_WIDTH - 1, D_CONV), x.dtype)
    yc, az, gg, cs, qt, iqt, kt, vt, kit, iwt, k_rows, ikw_rows, vt_blocks = _project(
        x, prev, *proj_w, tm=tm_proj, tokens_on_lanes=True)
    attn = _attn_prompt(qt, iqt, iwt, k_rows, ikw_rows, vt_blocks, tq=tq, cks=cks)
    flat = lambda a: a.reshape(b * t, a.shape[-1])
    y = _merge(flat(x), flat(yc), flat(attn), flat(az), flat(gg), *merge_w, tm=tm_merge)
    heads = lambda a: a.reshape(b, N_KV_HEADS, HEAD_DIM, t).transpose(0, 3, 1, 2)[None]
    return y.reshape(b, t, D_MODEL), heads(kt), heads(vt), kit.transpose(0, 2, 1)[None], cs[None]


def _sample_layer(x, prev, page_table, cache_k, cache_v, cache_kidx, proj_w, merge_w,
                  *, tm_proj, tm_merge, kc, group):
    b, t, _ = x.shape
    n_phys = cache_k.shape[0]
    zeros = jnp.zeros((b, t, D_CONV), prev.dtype)
    one_back = zeros.at[:, 0].set(prev[:, 1])
    two_back = zeros.at[:, 0].set(prev[:, 0]).at[:, 1].set(prev[:, 1])
    tiles = (b * t) // tm_proj
    tiled = lambda a: a.reshape(tiles, tm_proj, a.shape[-1])
    outs = _project(tiled(x), jnp.stack([tiled(one_back), tiled(two_back)], axis=1), *proj_w,
                    tm=tm_proj, tokens_on_lanes=False, rows_per_seq=t)
    yc, az, gg, u, q, iq, ikw, k, v = [a.reshape(b, t, a.shape[-1]) for a in outs]
    cs = u[:, t - (CONV_WIDTH - 1):]
    cache_kt = cache_k.transpose(0, 2, 3, 1).reshape(n_phys, D_KV, PAGE_SIZE)
    cache_vt = cache_v.transpose(0, 2, 3, 1).reshape(n_phys, D_KV, PAGE_SIZE)
    cache_kit = cache_kidx.transpose(0, 2, 1)
    dist = _select_sample(page_table, iq, ikw, cache_kit, kc=kc, group=group)
    attn = _attn_sample(page_table, q, k, v, dist, cache_kt, cache_vt, kc=kc)
    flat = lambda a: a.reshape(b * t, a.shape[-1])
    y = _merge(flat(x), flat(yc), flat(attn), flat(az), flat(gg), *merge_w, tm=tm_merge)
    heads = lambda a: a.reshape(1, b, t, N_KV_HEADS, HEAD_DIM)
    return y.reshape(b, t, D_MODEL), heads(k), heads(v), ikw[None, :, :, :IDX_DIM], cs[None]


def _tiles(prompt_shape, sample_shape, n_pages):
    b, seq, _ = prompt_shape
    db, t_new, _ = sample_shape
    row_tile = 2 * KEY_BLOCK
    tiles = dict(
        prompt=dict(tm_proj=min(row_tile, seq), tm_merge=min(row_tile, b * seq), tq=KEY_BLOCK,
                    cks=min(row_tile, seq)),
        sample=dict(tm_proj=min(row_tile, db * t_new), tm_merge=min(row_tile, db * t_new),
                    kc=min(8 * LANES, n_pages * PAGE_SIZE), group=min(8, db)))
    assert seq % tiles["prompt"]["tm_proj"] == 0 and seq % KEY_BLOCK == 0 and seq % tiles["prompt"]["cks"] == 0
    assert (b * seq) % tiles["prompt"]["tm_merge"] == 0
    assert (db * t_new) % tiles["sample"]["tm_proj"] == 0 and tiles["sample"]["tm_proj"] % t_new == 0
    assert (n_pages * PAGE_SIZE) % tiles["sample"]["kc"] == 0 and db % tiles["sample"]["group"] == 0
    assert min(TOPK_MAX, seq // 4) <= KEY_BLOCK * (seq // KEY_BLOCK)
    return tiles


def kernel(x_prompt, x_sample, cache_k, cache_v, cache_kidx, state_conv, page_table, norm_gain, w_in,
           conv_w, w_up_conv, w_up_attn, w_out, final_gain):
    assert norm_gain.shape[0] == 1, "single-layer step"
    proj_w, merge_w = _prepare_weights(norm_gain[0], w_in[0], conv_w[0], w_up_conv[0], w_up_attn[0],
                                       w_out[0], final_gain)
    tiles = _tiles(x_prompt.shape, x_sample.shape, page_table.shape[1])
    yp, kp, vp, ip, cp = _prompt_layer(x_prompt, proj_w, merge_w, **tiles["prompt"])
    ys, ks, vs, is_, cs = _sample_layer(x_sample, state_conv[0], page_table, cache_k[0], cache_v[0],
                                        cache_kidx[0], proj_w, merge_w, **tiles["sample"])
    return (yp, ys, kp, vp, ip, cp, ks, vs, is_, cs)
```

```python
import functools

import jax
import jax.numpy as jnp
from jax import lax
from jax.experimental import pallas as pl
from jax.experimental.pallas import tpu as pltpu

D_MODEL = 1024
D_CONV = 512
CONV_WIDTH = 3
N_HEADS = 8
HEAD_DIM = 64
N_KV_HEADS = 2
GROUP = N_HEADS // N_KV_HEADS
D_ATT = N_HEADS * HEAD_DIM
D_KV = N_KV_HEADS * HEAD_DIM
N_IDX_HEADS = 8
IDX_DIM = 64
D_IQ = N_IDX_HEADS * IDX_DIM
TOPK_MAX = 256
PAGE_SIZE = 128
EPS = 1e-6

LANES = 128
SUBLANES = 8
VMEM_LIMIT = 48 * 1024 * 1024
MXU_DTYPE = jnp.bfloat16
GATE_DTYPE = jnp.bfloat16

C_CONV = 0
C_Q = 4 * D_CONV
C_K = C_Q + D_ATT
C_V = C_K + D_KV
C_AZ = C_V + D_KV
C_IQ = C_AZ + D_ATT
C_IKW = C_IQ + D_IQ
C_G = C_IKW + LANES
D_IN_RAW = C_IKW + IDX_DIM + N_IDX_HEADS + 2 * D_MODEL
D_IN_PAD = C_G + 2 * D_MODEL
R_Q = 0
R_IQ = R_Q + D_ATT
R_K = R_IQ + D_IQ
R_V = R_K + D_KV
R_IKW = R_V + D_KV
D_T = -(-(R_IKW + IDX_DIM + N_IDX_HEADS) // LANES) * LANES

MASKED_DIST = -1e33
MASKED_LOGIT = -1e30
LOG2E = 1.4426950408889634
N_LOG2E_PIECES = 4
POS_RADIX = 64
KEY_BLOCK = 2 * LANES
INT_MIN = -2 ** 31
KEY_NEG_INF = INT_MIN + 0x7FFFFF


def _dot(a, b):
    return jnp.dot(a, b, preferred_element_type=jnp.float32)


def _dot_nt(a, b):
    return lax.dot_general(a, b, (((1,), (1,)), ((), ())), preferred_element_type=jnp.float32)


def _mxu(x):
    return x.astype(MXU_DTYPE)


def _proj_kernel(x_ref, prev_ref, g_ref, w_ref, wt_ref, cw_ref, *refs, tm, tokens_on_lanes, rows_per_seq):
    yc_ref, az_ref, gg_ref, cs_ref = refs[:4]
    u_buf = refs[-1]
    t = pl.program_id(1)
    nt = pl.num_programs(1)
    x = x_ref[0]
    xn = x * lax.rsqrt(jnp.mean(x * x, axis=-1, keepdims=True) + EPS) * g_ref[...]
    xn = _mxu(xn)

    def proj(c0, width):
        return _dot(xn, w_ref[:, c0:c0 + width])

    if tokens_on_lanes:
        t_all = _dot_nt(wt_ref[...], xn)

    def proj_t(r0, height):
        return t_all[r0:r0 + height]

    u = proj(C_CONV + D_CONV, D_CONV) * proj(C_CONV + 2 * D_CONV, D_CONV)
    if rows_per_seq is None:
        @pl.when(t == 0)
        def _():
            u_buf[0:SUBLANES, :] = jnp.zeros((SUBLANES, D_CONV), jnp.float32)
            u_buf[SUBLANES - 2:SUBLANES, :] = prev_ref[0]

        @pl.when(t > 0)
        def _():
            u_buf[0:SUBLANES, :] = u_buf[tm:tm + SUBLANES, :]

        u_buf[SUBLANES:SUBLANES + tm, :] = u
        u_m1 = u_buf[SUBLANES - 1:SUBLANES - 1 + tm, :]
        u_m2 = u_buf[SUBLANES - 2:SUBLANES - 2 + tm, :]

        @pl.when(t == nt - 1)
        def _():
            cs_ref[0] = u_buf[tm + SUBLANES - 2:tm + SUBLANES, :]
    else:
        u_buf[0:SUBLANES, :] = jnp.zeros((SUBLANES, D_CONV), jnp.float32)
        u_buf[SUBLANES:SUBLANES + tm, :] = u
        row_in_seq = lax.broadcasted_iota(jnp.int32, (tm, D_CONV), 0) % rows_per_seq
        u_m1 = jnp.where(row_in_seq >= 1, u_buf[SUBLANES - 1:SUBLANES - 1 + tm, :], 0.0) + prev_ref[0, 0]
        u_m2 = jnp.where(row_in_seq >= 2, u_buf[SUBLANES - 2:SUBLANES - 2 + tm, :], 0.0) + prev_ref[0, 1]
        cs_ref[0] = u
    conv = cw_ref[0:1, :] * u_m2 + cw_ref[1:2, :] * u_m1 + cw_ref[2:3, :] * u
    yc_ref[0] = (proj(C_CONV, D_CONV) * conv * jax.nn.silu(proj(C_CONV + 3 * D_CONV, D_CONV))).astype(yc_ref.dtype)

    az_ref[0] = proj(C_AZ, D_ATT)
    gg_ref[0] = proj(C_G, 2 * D_MODEL).astype(gg_ref.dtype)
    if tokens_on_lanes:
        qt_ref, iqt_ref, kt_ref, vt_ref, kit_ref, iwt_ref, kr_ref, ikwr_ref, vtc_ref = refs[4:13]
        qt_ref[0] = _mxu(proj_t(R_Q, D_ATT) * (HEAD_DIM ** -0.5 * LOG2E))
        iqt_ref[0] = _mxu(proj_t(R_IQ, D_IQ) * (IDX_DIM ** -0.5))
        kt_ref[0] = proj_t(R_K, D_KV)
        vt = proj_t(R_V, D_KV)
        vt_ref[0] = vt
        for s in range(tm // KEY_BLOCK):
            vtc_ref[0, s] = _mxu(vt[:, s * KEY_BLOCK:(s + 1) * KEY_BLOCK])
        kit_ref[0] = proj_t(R_IKW, IDX_DIM)
        iwt_ref[0] = proj_t(R_IKW + IDX_DIM, N_IDX_HEADS)
        kr_ref[0, :, 0:D_KV] = _mxu(proj(C_K, D_KV))
        pos = t * tm + lax.broadcasted_iota(jnp.int32, (tm, LANES), 0)
        lane = lax.broadcasted_iota(jnp.int32, (tm, LANES), 1)
        feat = jnp.where(lane % 2 == 0, pos // POS_RADIX, pos % POS_RADIX)
        kr_ref[0, :, D_KV:D_KV + LANES] = _mxu(
            jnp.where(lane < 2 * N_LOG2E_PIECES, feat, 0).astype(jnp.float32))
        ikwr_ref[0] = _mxu(proj(C_IKW, LANES))
    else:
        q_ref, iq_ref, ikw_ref, k_ref, v_ref = refs[4:9]
        q_ref[0] = proj(C_Q, D_ATT) * (HEAD_DIM ** -0.5)
        iq_ref[0] = proj(C_IQ, D_IQ) * (IDX_DIM ** -0.5)
        ikw_ref[0] = proj(C_IKW, LANES)
        k_ref[0] = proj(C_K, D_KV)
        v_ref[0] = proj(C_V, D_KV)


def _project(x, prev, norm_gain, w_pad, w_t, conv_w, *, tm, tokens_on_lanes, rows_per_seq=None):
    b, t, _ = x.shape
    nt = t // tm
    row = lambda width: pl.BlockSpec((1, tm, width), lambda i, j: (i, j, 0))
    col = lambda height: pl.BlockSpec((1, height, tm), lambda i, j: (i, 0, j))
    full = lambda shape: pl.BlockSpec(shape, lambda i, j: (0,) * len(shape), pipeline_mode=pl.Buffered(1))
    f32 = jnp.float32
    sds = jax.ShapeDtypeStruct
    if rows_per_seq is None:
        state_in = state_out = pl.BlockSpec((1, CONV_WIDTH - 1, D_CONV), lambda i, j: (i, 0, 0))
        state_shape = sds((b, CONV_WIDTH - 1, D_CONV), f32)
    else:
        assert tm == t and t % rows_per_seq == 0 and rows_per_seq >= CONV_WIDTH - 1
        state_in = pl.BlockSpec((1, CONV_WIDTH - 1, t, D_CONV), lambda i, j: (i, 0, 0, 0))
        state_out, state_shape = row(D_CONV), sds((b, t, D_CONV), f32)
    out_shapes = [sds((b, t, D_CONV), MXU_DTYPE), sds((b, t, D_ATT), f32), sds((b, t, 2 * D_MODEL), GATE_DTYPE),
                  state_shape]
    out_specs = [row(D_CONV), row(D_ATT), row(2 * D_MODEL), state_out]
    if tokens_on_lanes:
        out_shapes += [sds((b, D_ATT, t), MXU_DTYPE), sds((b, D_IQ, t), MXU_DTYPE),
                       sds((b, D_KV, t), f32), sds((b, D_KV, t), f32), sds((b, IDX_DIM, t), f32),
                       sds((b, N_IDX_HEADS, t), f32),
                       sds((b, t, D_KV + LANES), MXU_DTYPE), sds((b, t, LANES), MXU_DTYPE),
                       sds((b, t // KEY_BLOCK, D_KV, KEY_BLOCK), MXU_DTYPE)]
        out_specs += [col(D_ATT), col(D_IQ), col(D_KV), col(D_KV), col(IDX_DIM), col(N_IDX_HEADS),
                      row(D_KV + LANES), row(LANES),
                      pl.BlockSpec((1, tm // KEY_BLOCK, D_KV, KEY_BLOCK), lambda i, j: (i, j, 0, 0))]
    else:
        out_shapes += [sds((b, t, D_ATT), f32), sds((b, t, D_IQ), f32), sds((b, t, LANES), f32),
                       sds((b, t, D_KV), f32), sds((b, t, D_KV), f32)]
        out_specs += [row(D_ATT), row(D_IQ), row(LANES), row(D_KV), row(D_KV)]
    return pl.pallas_call(
        functools.partial(_proj_kernel, tm=tm, tokens_on_lanes=tokens_on_lanes, rows_per_seq=rows_per_seq),
        grid=(b, nt),
        in_specs=[row(D_MODEL), state_in, full((1, D_MODEL)), full((D_MODEL, D_IN_PAD)),
                  full((D_T, D_MODEL)), full((CONV_WIDTH, D_CONV))],
        out_specs=out_specs,
        out_shape=out_shapes,
        scratch_shapes=[pltpu.VMEM((tm + SUBLANES, D_CONV), f32)],
        compiler_params=pltpu.CompilerParams(
            dimension_semantics=("arbitrary", "arbitrary"), vmem_limit_bytes=VMEM_LIMIT),
        name="proj",
    )(x, prev, norm_gain.reshape(1, D_MODEL), w_pad, w_t, conv_w)


def _merge_kernel(x_ref, yc_ref, ya_ref, gg_ref, wuc_ref, wua_ref, wo_ref, fg_ref, y_ref):
    ya = ya_ref[...]
    gc = gg_ref[:, 0:D_MODEL].astype(jnp.float32)
    ga = gg_ref[:, D_MODEL:2 * D_MODEL].astype(jnp.float32)
    m = (jax.nn.sigmoid(gc) * _dot(yc_ref[...], wuc_ref[...])
         + jax.nn.sigmoid(ga) * _dot(ya, wua_ref[...]))
    xo = x_ref[...] + _dot(_mxu(m), wo_ref[...])
    y_ref[...] = xo * lax.rsqrt(jnp.mean(xo * xo, axis=-1, keepdims=True) + EPS) * fg_ref[...]


def _merge(x, yc, ya, gg, wuc, wua, wo, final_gain, *, tm):
    n = x.shape[0]
    row = lambda width: pl.BlockSpec((tm, width), lambda i: (i, 0))
    full = lambda shape: pl.BlockSpec(shape, lambda i: (0,) * len(shape), pipeline_mode=pl.Buffered(1))
    return pl.pallas_call(
        _merge_kernel,
        grid=(n // tm,),
        in_specs=[row(D_MODEL), row(D_CONV), row(D_ATT), row(2 * D_MODEL),
                  full((D_CONV, D_MODEL)), full((D_ATT, D_MODEL)), full((D_MODEL, D_MODEL)),
                  full((1, D_MODEL))],
        out_specs=row(D_MODEL),
        out_shape=jax.ShapeDtypeStruct((n, D_MODEL), jnp.float32),
        compiler_params=pltpu.CompilerParams(
            dimension_semantics=("arbitrary",), vmem_limit_bytes=VMEM_LIMIT),
        name="merge",
    )(x, yc, ya, gg, wuc, wua, wo, final_gain.reshape(1, D_MODEL))


def _key_to_threshold(tu):
    key = tu ^ jnp.int32(INT_MIN)
    bits = key ^ ((key >> 31) & jnp.int32(0x7FFFFFFF))
    thr = lax.bitcast_convert_type(bits, jnp.float32)
    return jnp.where(key < jnp.int32(KEY_NEG_INF), -jnp.inf, thr)


def _lane_column(x, lane):
    ids = lax.broadcasted_iota(jnp.int32, x.shape, 1)
    return jnp.sum(jnp.where(ids == lane, x, 0.0), axis=1, keepdims=True)


def _swap_halves(x):
    return pltpu.roll(x, LANES // 2, axis=1)


def _head_slope(h):
    return 2.0 ** (-8.0 * (h + 1) / N_HEADS)


def _pack_head_pair(a0, a1, kv_head):
    lo = lax.broadcasted_iota(jnp.int32, a0.shape, 1) < LANES // 2
    if kv_head == 0:
        return jnp.where(lo, a0, _swap_halves(a1))
    return jnp.where(lo, _swap_halves(a0), a1)


N_COUNT_ACC = 4
EXIT_FIRST_STEP = 16
EXIT_CHECK_STEPS = 4

def _alibi_rows(tq):
    pieces, rest = [], jnp.float32(LOG2E)
    for _ in range(N_LOG2E_PIECES):
        piece = rest.astype(MXU_DTYPE).astype(jnp.float32)
        pieces.append(piece)
        rest = rest - piece
    rows = jnp.zeros((N_HEADS, LANES), jnp.float32)
    for h in range(N_HEADS):
        for j, piece in enumerate(pieces):
            rows = rows.at[h, 2 * j].set(_head_slope(h) * POS_RADIX * piece)
            rows = rows.at[h, 2 * j + 1].set(_head_slope(h) * piece)
    return _mxu(jnp.broadcast_to(rows[:, :, None], (N_HEADS, LANES, tq)))


def _attn_prompt_kernel(qt_ref, iqt_ref, iwt_ref, az_ref, kr_ref, ikwr_ref, vtc_ref, aw_ref, o_ref,
                        qw_s, iqw_s, sc_s, mask_s, m_s, l_s, a_s, acc_s, j_s, ot_s, s_s, p_s,
                        *, seq, tq, cks, k_sel):
    i = pl.program_id(1)
    n_kb = (i + 1) * (tq // KEY_BLOCK)
    n_cks = ((i + 1) * tq + cks - 1) // cks
    f32 = jnp.float32

    zeros = jnp.zeros((HEAD_DIM, tq), MXU_DTYPE)
    for h in range(N_HEADS):
        g = h // GROUP
        qw_s[h, g * HEAD_DIM:(g + 1) * HEAD_DIM, :] = qt_ref[0, h * HEAD_DIM:(h + 1) * HEAD_DIM, :]
        qw_s[h, (1 - g) * HEAD_DIM:(2 - g) * HEAD_DIM, :] = zeros
        qw_s[h, D_KV:D_KV + LANES, :] = aw_ref[h]
        iqw_s[h, 0:IDX_DIM, :] = iqt_ref[0, h * IDX_DIM:(h + 1) * IDX_DIM, :]
        iqw_s[h, IDX_DIM:2 * IDX_DIM, :] = zeros

    w_rows = iwt_ref[0] * (N_IDX_HEADS ** -0.5)
    q_pos_c = i * tq + lax.broadcasted_iota(jnp.int32, (cks, tq), 1)
    k_row_c = lax.broadcasted_iota(jnp.int32, (cks, tq), 0)

    def score_chunk(c, carry):
        kk = ikwr_ref[0, pl.ds(pl.multiple_of(c * cks, cks), cks), :]
        acc = jnp.zeros((cks, tq), f32)
        for h in range(N_IDX_HEADS):
            acc = acc + jnp.maximum(_dot(kk, iqw_s[h]), 0.0) * w_rows[h:h + 1, :]
        acc = jnp.where(c * cks + k_row_c <= q_pos_c, acc, -jnp.inf)
        for t in range(cks // KEY_BLOCK):
            sc_s[c * (cks // KEY_BLOCK) + t] = acc[t * KEY_BLOCK:(t + 1) * KEY_BLOCK]
        return carry
    lax.fori_loop(0, n_cks, score_chunk, 0)

    sub_id = lax.broadcasted_iota(jnp.int32, (SUBLANES, tq), 0)

    def count(pred):
        def body(j, parts):
            parts = list(parts)
            for r in range(KEY_BLOCK // SUBLANES):
                rows = sc_s[j, r * SUBLANES:(r + 1) * SUBLANES, :]
                hit = pred(rows, j * KEY_BLOCK + r * SUBLANES + sub_id)
                parts[r % N_COUNT_ACC] = parts[r % N_COUNT_ACC] + jnp.where(hit, 1.0, 0.0)
            return tuple(parts)
        parts = lax.fori_loop(0, n_kb, body, (jnp.zeros((SUBLANES, tq), f32),) * N_COUNT_ACC)
        return jnp.sum(functools.reduce(lambda a, b: a + b, parts), axis=0, keepdims=True)

    def rows8(row):
        return jnp.broadcast_to(row, (SUBLANES, tq))

    def bisect(it, carry):
        tu, cnt_acc = carry
        cand = tu | jnp.left_shift(jnp.int32(1), 31 - it)
        thr8 = rows8(_key_to_threshold(cand))
        cnt = count(lambda s, kp: s >= thr8)
        ok = cnt >= float(k_sel)
        return jnp.where(ok, cand, tu), jnp.where(ok, cnt, cnt_acc)

    n_all = (n_kb * KEY_BLOCK).astype(f32)
    carry = (jnp.zeros((1, tq), jnp.int32), jnp.zeros((1, tq), f32) + n_all)
    carry = lax.fori_loop(0, EXIT_FIRST_STEP, bisect, carry)

    def unresolved(state):
        it, _, cnt = state
        return (it < 32) & (jnp.max(jnp.abs(cnt - float(k_sel))) > 0.0)

    def low_steps(state):
        it, tu, cnt = state
        tu, cnt = lax.fori_loop(it, it + EXIT_CHECK_STEPS, bisect, (tu, cnt))
        return it + EXIT_CHECK_STEPS, tu, cnt

    _, tu, cnt_ge = lax.while_loop(unresolved, low_steps, (jnp.int32(EXIT_FIRST_STEP),) + carry)
    thr = _key_to_threshold(tu)

    j_s[...] = jnp.full((1, tq), seq, jnp.int32)

    @pl.when(jnp.max(cnt_ge) > float(k_sel))
    def _():
        thr8 = rows8(thr)
        need = float(k_sel) - count(lambda s, kp: s > thr8)
        nbits = (seq - 1).bit_length()
        def index_bit(it, p):
            cand = p + jnp.left_shift(jnp.int32(1), nbits - 1 - it)
            cand8 = rows8(cand)
            below = count(lambda s, kp: (s == thr8) & (kp < cand8))
            return jnp.where(below < need, cand, p)
        j_s[...] = lax.fori_loop(0, nbits, index_bit, jnp.zeros((1, tq), jnp.int32))

    m_s[...] = jnp.full(m_s.shape, -jnp.inf, f32)
    l_s[...] = jnp.zeros(l_s.shape, f32)
    acc_s[...] = jnp.zeros(acc_s.shape, f32)
    j_last = j_s[...]
    q_pos = i * tq + lax.broadcasted_iota(jnp.int32, (KEY_BLOCK, tq), 1)
    k_row = lax.broadcasted_iota(jnp.int32, (KEY_BLOCK, tq), 0)

    def attend_block(j, buf):
        k_pos = j * KEY_BLOCK + k_row
        s_idx = sc_s[j]
        sel = ((s_idx > thr) | ((s_idx == thr) & (k_pos <= j_last))) & (k_pos <= q_pos)
        mask_s[buf] = jnp.where(sel, 0.0, MASKED_LOGIT)
        kk = kr_ref[0, pl.ds(pl.multiple_of(j * KEY_BLOCK, KEY_BLOCK), KEY_BLOCK), :]
        vt = vtc_ref[0, j]
        for h in range(N_HEADS):
            s_s[buf, h] = _dot(kk, qw_s[h]) + mask_s[buf]
        for h in range(N_HEADS):
            m_old = m_s[h:h + 1, :]
            m_new = jnp.maximum(m_old, jnp.max(s_s[buf, h], axis=0, keepdims=True))
            alpha = jnp.exp2(m_old - m_new)
            p = jnp.exp2(s_s[buf, h] - m_new)
            l_s[h:h + 1, :] = alpha * l_s[h:h + 1, :] + jnp.sum(p, axis=0, keepdims=True)
            p_s[buf, h] = _mxu(p)
            m_s[h:h + 1, :] = m_new
            a_s[buf, h:h + 1, :] = alpha
        for h in range(N_HEADS):
            g = h // GROUP
            acc_s[h] = (a_s[buf, h:h + 1, :] * acc_s[h]
                        + _dot(vt[g * HEAD_DIM:(g + 1) * HEAD_DIM, :], p_s[buf, h]))

    def attend_pair(t, carry):
        attend_block(2 * t, 0)
        attend_block(2 * t + 1, 1)
        return carry
    lax.fori_loop(0, n_kb // 2, attend_pair, 0)

    @pl.when(n_kb % 2 == 1)
    def _():
        attend_block(n_kb - 1, 0)

    for h in range(N_HEADS):
        ot_s[h * HEAD_DIM:(h + 1) * HEAD_DIM, :] = acc_s[h] / l_s[h:h + 1, :]
    o_ref[0] = _mxu(ot_s[...].T * jax.nn.silu(az_ref[0]))


def _attn_prompt(qt, iqt, iwt, az, k_rows, ikw_rows, vt_blocks, *, tq, cks):
    b, _, seq = qt.shape
    n_kb = seq // KEY_BLOCK
    k_sel = min(TOPK_MAX, seq // 4)
    colblk = lambda height: pl.BlockSpec((1, height, tq), lambda i, j: (i, 0, j))
    whole = lambda shape: pl.BlockSpec((1,) + shape, lambda i, j: (i,) + (0,) * len(shape))
    f32 = jnp.float32
    return pl.pallas_call(
        functools.partial(_attn_prompt_kernel, seq=seq, tq=tq, cks=cks, k_sel=k_sel),
        grid=(b, seq // tq),
        in_specs=[colblk(D_ATT), colblk(D_IQ), colblk(N_IDX_HEADS),
                  pl.BlockSpec((1, tq, D_ATT), lambda i, j: (i, j, 0)),
                  whole((seq, D_KV + LANES)), whole((seq, LANES)), whole((n_kb, D_KV, KEY_BLOCK)),
                  pl.BlockSpec((N_HEADS, LANES, tq), lambda i, j: (0, 0, 0), pipeline_mode=pl.Buffered(1))],
        out_specs=pl.BlockSpec((1, tq, D_ATT), lambda i, j: (i, j, 0)),
        out_shape=jax.ShapeDtypeStruct((b, seq, D_ATT), MXU_DTYPE),
        scratch_shapes=[
            pltpu.VMEM((N_HEADS, D_KV + LANES, tq), MXU_DTYPE),
            pltpu.VMEM((N_IDX_HEADS, LANES, tq), MXU_DTYPE),
            pltpu.VMEM((n_kb, KEY_BLOCK, tq), f32),
            pltpu.VMEM((2, KEY_BLOCK, tq), f32),
            pltpu.VMEM((N_HEADS, tq), f32),
            pltpu.VMEM((N_HEADS, tq), f32),
            pltpu.VMEM((2, N_HEADS, tq), f32),
            pltpu.VMEM((N_HEADS, HEAD_DIM, tq), f32),
            pltpu.VMEM((1, tq), jnp.int32),
            pltpu.VMEM((D_ATT, tq), f32),
            pltpu.VMEM((2, N_HEADS, KEY_BLOCK, tq), f32),
            pltpu.VMEM((2, N_HEADS, KEY_BLOCK, tq), MXU_DTYPE),
        ],
        compiler_params=pltpu.CompilerParams(
            dimension_semantics=("arbitrary", "arbitrary"), vmem_limit_bytes=VMEM_LIMIT),
        name="attn_prompt",
    )(qt, iqt, iwt, az, k_rows, ikw_rows, vt_blocks, _alibi_rows(tq))


def _select_sample_kernel(pt_ref, iq_ref, ikw_ref, ci_hbm, dist_ref, ibuf, sc_s, sem,
                          *, n_pages, t_new, kc, k_sel, group):
    step = pl.program_id(0)
    n_steps = pl.num_programs(0)
    slot = step % 2
    past = n_pages * PAGE_SIZE
    rows = group * t_new
    f32 = jnp.float32

    def page_copy(at_step, at_slot, g, p):
        phys = pt_ref[at_step * group + g, p]
        cols = pl.ds(pl.multiple_of(p * PAGE_SIZE, PAGE_SIZE), PAGE_SIZE)
        return pltpu.make_async_copy(ci_hbm.at[phys], ibuf.at[at_slot, g, :, cols], sem.at[at_slot])

    def start_pages(at_step, at_slot):
        def start_page(p, carry):
            for g in range(group):
                page_copy(at_step, at_slot, g, p).start()
            return carry
        lax.fori_loop(0, n_pages, start_page, 0)

    @pl.when(step == 0)
    def _():
        start_pages(0, 0)

    @pl.when(step + 1 < n_steps)
    def _():
        start_pages(step + 1, 1 - slot)

    def head_sum(r):
        acc = r[0:t_new]
        for h in range(1, N_HEADS):
            acc = acc + r[h * t_new:(h + 1) * t_new]
        return acc

    operands = []
    for g in range(group):
        ikw = ikw_ref[g]
        iq_rows, w_rows = [], []
        for h in range(N_IDX_HEADS):
            it = iq_ref[g, :, (h // 2) * LANES:(h // 2 + 1) * LANES]
            iq_rows.append((it if h % 2 == 0 else _swap_halves(it))[:, 0:IDX_DIM])
            w_rows.append(_lane_column(ikw, IDX_DIM + h) * (N_IDX_HEADS ** -0.5))
        pad = jnp.zeros((LANES - t_new, LANES), f32)
        ik_new = _mxu(jnp.concatenate([ikw, pad], axis=0)[:, 0:IDX_DIM])
        operands.append((_mxu(jnp.concatenate(iq_rows, axis=0)), jnp.concatenate(w_rows, axis=0), ik_new))

    def wait_page(p, carry):
        for g in range(group):
            page_copy(step, slot, g, p).wait()
        return carry
    lax.fori_loop(0, n_pages, wait_page, 0)

    t_id = lax.broadcasted_iota(jnp.int32, (t_new, LANES), 0)
    j_id = lax.broadcasted_iota(jnp.int32, (t_new, LANES), 1)
    for g, (iq_all, w_all, ik_new) in enumerate(operands):
        seq_rows = slice(g * t_new, (g + 1) * t_new)
        for c in range(past // kc):
            cols = slice(c * kc, (c + 1) * kc)
            sc_s[seq_rows, cols] = head_sum(
                jnp.maximum(_dot(iq_all, _mxu(ibuf[slot, g, :, cols])), 0.0) * w_all)
        s_new = head_sum(jnp.maximum(_dot_nt(iq_all, ik_new), 0.0) * w_all)
        sc_s[seq_rows, past:past + LANES] = jnp.where(j_id <= t_id, s_new, -jnp.inf)

    sc = sc_s[...]
    def count(mask):
        return jnp.sum(jnp.where(mask, 1.0, 0.0), axis=1, keepdims=True)

    def bisect(it, carry):
        tu, cnt_acc = carry
        shift = 30 - 2 * it
        for digit in (1, 2, 3):
            cand = tu | jnp.left_shift(jnp.int32(digit), shift)
            cnt = count(sc >= _key_to_threshold(cand))
            ok = cnt >= float(k_sel)
            best = jnp.where(ok, cand, tu if digit == 1 else best)
            cnt_acc = jnp.where(ok, cnt, cnt_acc)
        return best, cnt_acc

    tu, cnt_ge = lax.fori_loop(
        0, 16, bisect,
        (jnp.zeros((rows, 1), jnp.int32), jnp.full((rows, 1), float(past + LANES), f32)))
    thr = _key_to_threshold(tu)

    k_pos = lax.broadcasted_iota(jnp.int32, sc.shape, 1)
    q_pos = past + lax.broadcasted_iota(jnp.int32, sc.shape, 0) % t_new
    nbits = (past + LANES - 1).bit_length()

    def tie_limit():
        need = float(k_sel) - count(sc > thr)
        def index_bit(it, p):
            cand = p + jnp.left_shift(jnp.int32(1), nbits - 1 - it)
            below = count((sc == thr) & (k_pos < cand))
            return jnp.where(below < need, cand, p)
        return lax.fori_loop(0, nbits, index_bit, jnp.zeros((rows, 1), jnp.int32))

    j_last = lax.cond(jnp.max(cnt_ge) > float(k_sel), tie_limit,
                      lambda: jnp.full((rows, 1), past + LANES, jnp.int32))
    sel = ((sc > thr) | ((sc == thr) & (k_pos <= j_last))) & (k_pos <= q_pos)
    dist = jnp.where(sel, (k_pos - q_pos).astype(f32), MASKED_DIST)
    for g in range(group):
        dist_ref[g] = dist[g * t_new:(g + 1) * t_new]


def _select_sample(page_table, iq, ikw, cache_kit, *, kc, group):
    db, t_new, _ = iq.shape
    n_pages = page_table.shape[1]
    past = n_pages * PAGE_SIZE
    k_sel = min(TOPK_MAX, (past + t_new) // 4)
    blk = lambda width: pl.BlockSpec((group, t_new, width), lambda i, pt: (i, 0, 0))
    f32 = jnp.float32
    return pl.pallas_call(
        functools.partial(_select_sample_kernel, n_pages=n_pages, t_new=t_new, kc=kc, k_sel=k_sel,
                          group=group),
        grid_spec=pltpu.PrefetchScalarGridSpec(
            num_scalar_prefetch=1,
            grid=(db // group,),
            in_specs=[blk(D_IQ), blk(LANES), pl.BlockSpec(memory_space=pl.ANY)],
            out_specs=blk(past + LANES),
            scratch_shapes=[
                pltpu.VMEM((2, group, IDX_DIM, past), f32),
                pltpu.VMEM((group * t_new, past + LANES), f32),
                pltpu.SemaphoreType.DMA((2,)),
            ]),
        out_shape=jax.ShapeDtypeStruct((db, t_new, past + LANES), f32),
        compiler_params=pltpu.CompilerParams(
            dimension_semantics=("arbitrary",), vmem_limit_bytes=VMEM_LIMIT),
        name="select_sample",
    )(page_table, iq, ikw, cache_kit)


def _attn_sample_kernel(pt_ref, q_ref, k_ref, v_ref, az_ref, dist_ref, ck_hbm, cv_hbm, o_ref,
                        kbuf, vbuf, s_s, sem,
                        *, n_pages, t_new, kc):
    b = pl.program_id(0)
    nb = pl.num_programs(0)
    slot = b % 2
    past = n_pages * PAGE_SIZE
    n_rows = N_HEADS * t_new
    f32 = jnp.float32

    def page_copies(seq_id, seq_slot, p):
        phys = pt_ref[seq_id, p]
        cols = pl.ds(pl.multiple_of(p * PAGE_SIZE, PAGE_SIZE), PAGE_SIZE)
        return (pltpu.make_async_copy(ck_hbm.at[phys], kbuf.at[seq_slot, :, cols], sem.at[seq_slot, 0]),
                pltpu.make_async_copy(cv_hbm.at[phys], vbuf.at[seq_slot, :, cols], sem.at[seq_slot, 1]))

    def start_pages(seq_id, seq_slot):
        def start_page(p, carry):
            for cp in page_copies(seq_id, seq_slot, p):
                cp.start()
            return carry
        lax.fori_loop(0, n_pages, start_page, 0)

    @pl.when(b == 0)
    def _():
        start_pages(0, 0)

    @pl.when(b + 1 < nb)
    def _():
        start_pages(b + 1, 1 - slot)

    lo = lax.broadcasted_iota(jnp.int32, (t_new, LANES), 1) < LANES // 2
    q_rows = []
    for h in range(N_HEADS):
        qt = q_ref[0, :, (h // 2) * LANES:(h // 2 + 1) * LANES]
        g, par = h // GROUP, h % 2
        if g != par:
            qt = _swap_halves(qt)
        q_rows.append(jnp.where(lo, qt, 0.0) if g == 0 else jnp.where(lo, 0.0, qt))
    q_all = _mxu(jnp.concatenate(q_rows, axis=0))
    slope = jnp.concatenate(
        [jnp.full((t_new, 1), _head_slope(h), f32) for h in range(N_HEADS)], axis=0)

    pad = jnp.zeros((LANES - t_new, LANES), f32)
    k_new = _mxu(jnp.concatenate([k_ref[0], pad], axis=0))
    v_new = _mxu(jnp.concatenate([v_ref[0], pad], axis=0))

    def wait_page(p, carry):
        for cp in page_copies(b, slot, p):
            cp.wait()
        return carry
    lax.fori_loop(0, n_pages, wait_page, 0)

    dist = dist_ref[0]

    def logits(s, d):
        return s + slope * jnp.concatenate([d] * N_HEADS, axis=0)

    m = jnp.full((n_rows, 1), -jnp.inf, f32)
    for c in range(past // kc):
        cols = slice(c * kc, (c + 1) * kc)
        s = logits(_dot(q_all, _mxu(kbuf[slot, :, cols])), dist[:, cols])
        s_s[:, cols] = s
        m = jnp.maximum(m, jnp.max(s, axis=1, keepdims=True))
    s = logits(_dot_nt(q_all, k_new), dist[:, past:past + LANES])
    s_s[:, past:past + LANES] = s
    m = jnp.maximum(m, jnp.max(s, axis=1, keepdims=True))

    l = jnp.zeros((n_rows, 1), f32)
    o = jnp.zeros((n_rows, LANES), f32)
    for c in range(past // kc):
        cols = slice(c * kc, (c + 1) * kc)
        p = jnp.exp(s_s[:, cols] - m)
        l = l + jnp.sum(p, axis=1, keepdims=True)
        o = o + _dot_nt(_mxu(p), _mxu(vbuf[slot, :, cols]))
    p = jnp.exp(s_s[:, past:past + LANES] - m)
    l = l + jnp.sum(p, axis=1, keepdims=True)
    o = (o + _dot(_mxu(p), v_new)) / l

    for j in range(N_HEADS // 2):
        tile = _pack_head_pair(
            o[(2 * j) * t_new:(2 * j + 1) * t_new], o[(2 * j + 1) * t_new:(2 * j + 2) * t_new],
            (2 * j) // GROUP)
        gate = jax.nn.silu(az_ref[0, :, j * LANES:(j + 1) * LANES])
        o_ref[0, :, j * LANES:(j + 1) * LANES] = _mxu(tile * gate)


def _attn_sample(page_table, q, k, v, az, dist, cache_kt, cache_vt, *, kc):
    db, t_new, _ = q.shape
    n_pages = page_table.shape[1]
    past = n_pages * PAGE_SIZE
    blk = lambda width: pl.BlockSpec((1, t_new, width), lambda i, pt: (i, 0, 0))
    hbm = pl.BlockSpec(memory_space=pl.ANY)
    f32 = jnp.float32
    return pl.pallas_call(
        functools.partial(_attn_sample_kernel, n_pages=n_pages, t_new=t_new, kc=kc),
        grid_spec=pltpu.PrefetchScalarGridSpec(
            num_scalar_prefetch=1,
            grid=(db,),
            in_specs=[blk(D_ATT), blk(D_KV), blk(D_KV), blk(D_ATT), blk(past + LANES), hbm, hbm],
            out_specs=blk(D_ATT),
            scratch_shapes=[
                pltpu.VMEM((2, D_KV, past), f32),
                pltpu.VMEM((2, D_KV, past), f32),
                pltpu.VMEM((N_HEADS * t_new, past + LANES), f32),
                pltpu.SemaphoreType.DMA((2, 2)),
            ]),
        out_shape=jax.ShapeDtypeStruct((db, t_new, D_ATT), MXU_DTYPE),
        compiler_params=pltpu.CompilerParams(
            dimension_semantics=("arbitrary",), vmem_limit_bytes=VMEM_LIMIT),
        name="attn_sample",
    )(page_table, q, k, v, az, dist, cache_kt, cache_vt)


def _prepare_weights(norm_gain, w_in, conv_w, w_up_conv, w_up_attn, w_out, final_gain):
    split = C_IKW + IDX_DIM + N_IDX_HEADS
    pad = jnp.zeros((D_MODEL, D_IN_PAD - D_IN_RAW), w_in.dtype)
    w_pad = _mxu(jnp.concatenate([w_in[:, :split], pad, w_in[:, split:]], axis=1))
    used_t = R_IKW + IDX_DIM + N_IDX_HEADS
    w_in_t = w_in.T
    w_t = _mxu(jnp.concatenate(
        [w_in_t[C_Q:C_Q + D_ATT], w_in_t[C_IQ:C_IQ + D_IQ], w_in_t[C_K:C_K + 2 * D_KV],
         w_in_t[C_IKW:split], jnp.zeros((D_T - used_t, D_MODEL), w_in.dtype)], axis=0))
    return (norm_gain, w_pad, w_t, conv_w), (_mxu(w_up_conv), _mxu(w_up_attn), _mxu(w_out), final_gain)


def _prompt_layer(x, proj_w, merge_w, *, tm_proj, tm_merge, tq, cks):
    b, t, _ = x.shape
    prev = jnp.zeros((b, CONV_WIDTH - 1, D_CONV), x.dtype)
    yc, az, gg, cs, qt, iqt, kt, vt, kit, iwt, k_rows, ikw_rows, vt_blocks = _project(
        x, prev, *proj_w, tm=tm_proj, tokens_on_lanes=True)
    ya = _attn_prompt(qt, iqt, iwt, az, k_rows, ikw_rows, vt_blocks, tq=tq, cks=cks)
    flat = lambda a: a.reshape(b * t, a.shape[-1])
    y = _merge(flat(x), flat(yc), flat(ya), flat(gg), *merge_w, tm=tm_merge)
    heads = lambda a: a.reshape(b, N_KV_HEADS, HEAD_DIM, t).transpose(0, 3, 1, 2)[None]
    return y.reshape(b, t, D_MODEL), heads(kt), heads(vt), kit.transpose(0, 2, 1)[None], cs[None]


def _sample_layer(x, prev, page_table, cache_k, cache_v, cache_kidx, proj_w, merge_w,
                  *, tm_proj, tm_merge, kc, group):
    b, t, _ = x.shape
    n_phys = cache_k.shape[0]
    zeros = jnp.zeros((b, t, D_CONV), prev.dtype)
    one_back = zeros.at[:, 0].set(prev[:, 1])
    two_back = zeros.at[:, 0].set(prev[:, 0]).at[:, 1].set(prev[:, 1])
    tiles = (b * t) // tm_proj
    tiled = lambda a: a.reshape(tiles, tm_proj, a.shape[-1])
    outs = _project(tiled(x), jnp.stack([tiled(one_back), tiled(two_back)], axis=1), *proj_w,
                    tm=tm_proj, tokens_on_lanes=False, rows_per_seq=t)
    yc, az, gg, u, q, iq, ikw, k, v = [a.reshape(b, t, a.shape[-1]) for a in outs]
    cs = u[:, t - (CONV_WIDTH - 1):]
    cache_kt = cache_k.transpose(0, 2, 3, 1).reshape(n_phys, D_KV, PAGE_SIZE)
    cache_vt = cache_v.transpose(0, 2, 3, 1).reshape(n_phys, D_KV, PAGE_SIZE)
    cache_kit = cache_kidx.transpose(0, 2, 1)
    dist = _select_sample(page_table, iq, ikw, cache_kit, kc=kc, group=group)
    ya = _attn_sample(page_table, q, k, v, az, dist, cache_kt, cache_vt, kc=kc)
    flat = lambda a: a.reshape(b * t, a.shape[-1])
    y = _merge(flat(x), flat(yc), flat(ya), flat(gg), *merge_w, tm=tm_merge)
    heads = lambda a: a.reshape(1, b, t, N_KV_HEADS, HEAD_DIM)
    return y.reshape(b, t, D_MODEL), heads(k), heads(v), ikw[None, :, :, :IDX_DIM], cs[None]


def _tiles(prompt_shape, sample_shape, n_pages):
    b, seq, _ = prompt_shape
    db, t_new, _ = sample_shape
    row_tile = 2 * KEY_BLOCK
    tiles = dict(
        prompt=dict(tm_proj=min(row_tile, seq), tm_merge=min(row_tile, b * seq), tq=KEY_BLOCK,
                    cks=min(row_tile, seq)),
        sample=dict(tm_proj=min(row_tile, db * t_new), tm_merge=min(row_tile, db * t_new),
                    kc=min(8 * LANES, n_pages * PAGE_SIZE), group=min(8, db)))
    assert seq % tiles["prompt"]["tm_proj"] == 0 and seq % KEY_BLOCK == 0 and seq % tiles["prompt"]["cks"] == 0
    assert (b * seq) % tiles["prompt"]["tm_merge"] == 0
    assert (db * t_new) % tiles["sample"]["tm_proj"] == 0 and tiles["sample"]["tm_proj"] % t_new == 0
    assert (n_pages * PAGE_SIZE) % tiles["sample"]["kc"] == 0 and db % tiles["sample"]["group"] == 0
    assert min(TOPK_MAX, seq // 4) <= KEY_BLOCK * (seq // KEY_BLOCK)
    return tiles


def kernel(x_prompt, x_sample, cache_k, cache_v, cache_kidx, state_conv, page_table, norm_gain, w_in,
           conv_w, w_up_conv, w_up_attn, w_out, final_gain):
    assert norm_gain.shape[0] == 1, "single-layer step"
    proj_w, merge_w = _prepare_weights(norm_gain[0], w_in[0], conv_w[0], w_up_conv[0], w_up_attn[0],
                                       w_out[0], final_gain)
    tiles = _tiles(x_prompt.shape, x_sample.shape, page_table.shape[1])
    yp, kp, vp, ip, cp = _prompt_layer(x_prompt, proj_w, merge_w, **tiles["prompt"])
    ys, ks, vs, is_, cs = _sample_layer(x_sample, state_conv[0], page_table, cache_k[0], cache_v[0],
                                        cache_kidx[0], proj_w, merge_w, **tiles["sample"])
    return (yp, ys, kp, vp, ip, cp, ks, vs, is_, cs)
```

```python
import functools

import jax
import jax.numpy as jnp
from jax import lax
from jax.experimental import pallas as pl
from jax.experimental.pallas import tpu as pltpu

D_MODEL = 1024
D_CONV = 512
CONV_WIDTH = 3
N_HEADS = 8
HEAD_DIM = 64
N_KV_HEADS = 2
GROUP = N_HEADS // N_KV_HEADS
D_ATT = N_HEADS * HEAD_DIM
D_KV = N_KV_HEADS * HEAD_DIM
N_IDX_HEADS = 8
IDX_DIM = 64
D_IQ = N_IDX_HEADS * IDX_DIM
TOPK_MAX = 256
PAGE_SIZE = 128
EPS = 1e-6

LANES = 128
SUBLANES = 8
VMEM_LIMIT = 48 * 1024 * 1024
MXU_DTYPE = jnp.bfloat16
GATE_DTYPE = jnp.bfloat16

C_CONV = 0
C_Q = 4 * D_CONV
C_K = C_Q + D_ATT
C_V = C_K + D_KV
C_AZ = C_V + D_KV
C_IQ = C_AZ + D_ATT
C_IKW = C_IQ + D_IQ
C_G = C_IKW + LANES
D_IN_RAW = C_IKW + IDX_DIM + N_IDX_HEADS + 2 * D_MODEL
D_IN_PAD = C_G + 2 * D_MODEL
R_Q = 0
R_IQ = R_Q + D_ATT
R_K = R_IQ + D_IQ
R_V = R_K + D_KV
R_IKW = R_V + D_KV
D_T = -(-(R_IKW + IDX_DIM + N_IDX_HEADS) // LANES) * LANES

MASKED_DIST = -1e33
MASKED_LOGIT = -1e30
LOG2E = 1.4426950408889634
N_LOG2E_PIECES = 4
POS_RADIX = 64
KEY_BLOCK = 2 * LANES
INT_MIN = -2 ** 31
KEY_NEG_INF = INT_MIN + 0x7FFFFF


def _dot(a, b):
    return jnp.dot(a, b, preferred_element_type=jnp.float32)


def _dot_nt(a, b):
    return lax.dot_general(a, b, (((1,), (1,)), ((), ())), preferred_element_type=jnp.float32)


def _mxu(x):
    return x.astype(MXU_DTYPE)


def _proj_kernel(x_ref, prev_ref, g_ref, w_ref, wt_ref, cw_ref, *refs, tm, tokens_on_lanes, rows_per_seq):
    yc_ref, az_ref, gg_ref, cs_ref = refs[:4]
    u_buf = refs[-1]
    t = pl.program_id(1)
    nt = pl.num_programs(1)
    x = x_ref[0]
    xn = x * lax.rsqrt(jnp.mean(x * x, axis=-1, keepdims=True) + EPS) * g_ref[...]
    xn = _mxu(xn)

    def proj(c0, width):
        return _dot(xn, w_ref[:, c0:c0 + width])

    if tokens_on_lanes:
        t_all = _dot_nt(wt_ref[...], xn)

    def proj_t(r0, height):
        return t_all[r0:r0 + height]

    u = proj(C_CONV + D_CONV, D_CONV) * proj(C_CONV + 2 * D_CONV, D_CONV)
    if rows_per_seq is None:
        @pl.when(t == 0)
        def _():
            u_buf[0:SUBLANES, :] = jnp.zeros((SUBLANES, D_CONV), jnp.float32)
            u_buf[SUBLANES - 2:SUBLANES, :] = prev_ref[0]

        @pl.when(t > 0)
        def _():
            u_buf[0:SUBLANES, :] = u_buf[tm:tm + SUBLANES, :]

        u_buf[SUBLANES:SUBLANES + tm, :] = u
        u_m1 = u_buf[SUBLANES - 1:SUBLANES - 1 + tm, :]
        u_m2 = u_buf[SUBLANES - 2:SUBLANES - 2 + tm, :]

        @pl.when(t == nt - 1)
        def _():
            cs_ref[0] = u_buf[tm + SUBLANES - 2:tm + SUBLANES, :]
    else:
        u_buf[0:SUBLANES, :] = jnp.zeros((SUBLANES, D_CONV), jnp.float32)
        u_buf[SUBLANES:SUBLANES + tm, :] = u
        row_in_seq = lax.broadcasted_iota(jnp.int32, (tm, D_CONV), 0) % rows_per_seq
        u_m1 = jnp.where(row_in_seq >= 1, u_buf[SUBLANES - 1:SUBLANES - 1 + tm, :], 0.0) + prev_ref[0, 0]
        u_m2 = jnp.where(row_in_seq >= 2, u_buf[SUBLANES - 2:SUBLANES - 2 + tm, :], 0.0) + prev_ref[0, 1]
        cs_ref[0] = u
    conv = cw_ref[0:1, :] * u_m2 + cw_ref[1:2, :] * u_m1 + cw_ref[2:3, :] * u
    yc_ref[0] = (proj(C_CONV, D_CONV) * conv * jax.nn.silu(proj(C_CONV + 3 * D_CONV, D_CONV))).astype(yc_ref.dtype)

    az_ref[0] = proj(C_AZ, D_ATT)
    gg_ref[0] = proj(C_G, 2 * D_MODEL).astype(gg_ref.dtype)
    if tokens_on_lanes:
        qt_ref, iqt_ref, kt_ref, vt_ref, kit_ref, iwt_ref, kr_ref, ikwr_ref, vtc_ref = refs[4:13]
        qt_ref[0] = _mxu(proj_t(R_Q, D_ATT) * (HEAD_DIM ** -0.5 * LOG2E))
        iqt_ref[0] = _mxu(proj_t(R_IQ, D_IQ) * (IDX_DIM ** -0.5))
        kt_ref[0] = proj_t(R_K, D_KV)
        vt = proj_t(R_V, D_KV)
        vt_ref[0] = vt
        for s in range(tm // KEY_BLOCK):
            vtc_ref[0, s] = _mxu(vt[:, s * KEY_BLOCK:(s + 1) * KEY_BLOCK])
        kit_ref[0] = proj_t(R_IKW, IDX_DIM)
        iwt_ref[0] = proj_t(R_IKW + IDX_DIM, N_IDX_HEADS)
        kr_ref[0, :, 0:D_KV] = _mxu(proj(C_K, D_KV))
        pos = t * tm + lax.broadcasted_iota(jnp.int32, (tm, LANES), 0)
        lane = lax.broadcasted_iota(jnp.int32, (tm, LANES), 1)
        feat = jnp.where(lane % 2 == 0, pos // POS_RADIX, pos % POS_RADIX)
        kr_ref[0, :, D_KV:D_KV + LANES] = _mxu(
            jnp.where(lane < 2 * N_LOG2E_PIECES, feat, 0).astype(jnp.float32))
        ikwr_ref[0] = _mxu(proj(C_IKW, LANES))
    else:
        q_ref, iq_ref, ikw_ref, k_ref, v_ref = refs[4:9]
        q_ref[0] = proj(C_Q, D_ATT) * (HEAD_DIM ** -0.5)
        iq_ref[0] = proj(C_IQ, D_IQ) * (IDX_DIM ** -0.5)
        ikw_ref[0] = proj(C_IKW, LANES)
        k_ref[0] = proj(C_K, D_KV)
        v_ref[0] = proj(C_V, D_KV)


def _project(x, prev, norm_gain, w_pad, w_t, conv_w, *, tm, tokens_on_lanes, rows_per_seq=None):
    b, t, _ = x.shape
    nt = t // tm
    row = lambda width: pl.BlockSpec((1, tm, width), lambda i, j: (i, j, 0))
    col = lambda height: pl.BlockSpec((1, height, tm), lambda i, j: (i, 0, j))
    full = lambda shape: pl.BlockSpec(shape, lambda i, j: (0,) * len(shape), pipeline_mode=pl.Buffered(1))
    f32 = jnp.float32
    sds = jax.ShapeDtypeStruct
    if rows_per_seq is None:
        state_in = state_out = pl.BlockSpec((1, CONV_WIDTH - 1, D_CONV), lambda i, j: (i, 0, 0))
        state_shape = sds((b, CONV_WIDTH - 1, D_CONV), f32)
    else:
        assert tm == t and t % rows_per_seq == 0 and rows_per_seq >= CONV_WIDTH - 1
        state_in = pl.BlockSpec((1, CONV_WIDTH - 1, t, D_CONV), lambda i, j: (i, 0, 0, 0))
        state_out, state_shape = row(D_CONV), sds((b, t, D_CONV), f32)
    out_shapes = [sds((b, t, D_CONV), MXU_DTYPE), sds((b, t, D_ATT), f32), sds((b, t, 2 * D_MODEL), GATE_DTYPE),
                  state_shape]
    out_specs = [row(D_CONV), row(D_ATT), row(2 * D_MODEL), state_out]
    if tokens_on_lanes:
        out_shapes += [sds((b, D_ATT, t), MXU_DTYPE), sds((b, D_IQ, t), MXU_DTYPE),
                       sds((b, D_KV, t), f32), sds((b, D_KV, t), f32), sds((b, IDX_DIM, t), f32),
                       sds((b, N_IDX_HEADS, t), f32),
                       sds((b, t, D_KV + LANES), MXU_DTYPE), sds((b, t, LANES), MXU_DTYPE),
                       sds((b, t // KEY_BLOCK, D_KV, KEY_BLOCK), MXU_DTYPE)]
        out_specs += [col(D_ATT), col(D_IQ), col(D_KV), col(D_KV), col(IDX_DIM), col(N_IDX_HEADS),
                      row(D_KV + LANES), row(LANES),
                      pl.BlockSpec((1, tm // KEY_BLOCK, D_KV, KEY_BLOCK), lambda i, j: (i, j, 0, 0))]
    else:
        out_shapes += [sds((b, t, D_ATT), f32), sds((b, t, D_IQ), f32), sds((b, t, LANES), f32),
                       sds((b, t, D_KV), f32), sds((b, t, D_KV), f32)]
        out_specs += [row(D_ATT), row(D_IQ), row(LANES), row(D_KV), row(D_KV)]
    return pl.pallas_call(
        functools.partial(_proj_kernel, tm=tm, tokens_on_lanes=tokens_on_lanes, rows_per_seq=rows_per_seq),
        grid=(b, nt),
        in_specs=[row(D_MODEL), state_in, full((1, D_MODEL)), full((D_MODEL, D_IN_PAD)),
                  full((D_T, D_MODEL)), full((CONV_WIDTH, D_CONV))],
        out_specs=out_specs,
        out_shape=out_shapes,
        scratch_shapes=[pltpu.VMEM((tm + SUBLANES, D_CONV), f32)],
        compiler_params=pltpu.CompilerParams(
            dimension_semantics=("arbitrary", "arbitrary"), vmem_limit_bytes=VMEM_LIMIT),
        name="proj",
    )(x, prev, norm_gain.reshape(1, D_MODEL), w_pad, w_t, conv_w)


def _merge_kernel(x_ref, yc_ref, ya_ref, gg_ref, wuc_ref, wua_ref, wo_ref, fg_ref, y_ref):
    ya = ya_ref[...]
    gc = gg_ref[:, 0:D_MODEL].astype(jnp.float32)
    ga = gg_ref[:, D_MODEL:2 * D_MODEL].astype(jnp.float32)
    m = (jax.nn.sigmoid(gc) * _dot(yc_ref[...], wuc_ref[...])
         + jax.nn.sigmoid(ga) * _dot(ya, wua_ref[...]))
    xo = x_ref[...] + _dot(_mxu(m), wo_ref[...])
    y_ref[...] = xo * lax.rsqrt(jnp.mean(xo * xo, axis=-1, keepdims=True) + EPS) * fg_ref[...]


def _merge(x, yc, ya, gg, wuc, wua, wo, final_gain, *, tm):
    n = x.shape[0]
    row = lambda width: pl.BlockSpec((tm, width), lambda i: (i, 0))
    full = lambda shape: pl.BlockSpec(shape, lambda i: (0,) * len(shape), pipeline_mode=pl.Buffered(1))
    return pl.pallas_call(
        _merge_kernel,
        grid=(n // tm,),
        in_specs=[row(D_MODEL), row(D_CONV), row(D_ATT), row(2 * D_MODEL),
                  full((D_CONV, D_MODEL)), full((D_ATT, D_MODEL)), full((D_MODEL, D_MODEL)),
                  full((1, D_MODEL))],
        out_specs=row(D_MODEL),
        out_shape=jax.ShapeDtypeStruct((n, D_MODEL), jnp.float32),
        compiler_params=pltpu.CompilerParams(
            dimension_semantics=("arbitrary",), vmem_limit_bytes=VMEM_LIMIT),
        name="merge",
    )(x, yc, ya, gg, wuc, wua, wo, final_gain.reshape(1, D_MODEL))


def _key_to_threshold(tu):
    key = tu ^ jnp.int32(INT_MIN)
    bits = key ^ ((key >> 31) & jnp.int32(0x7FFFFFFF))
    thr = lax.bitcast_convert_type(bits, jnp.float32)
    return jnp.where(key < jnp.int32(KEY_NEG_INF), -jnp.inf, thr)


def _lane_column(x, lane):
    ids = lax.broadcasted_iota(jnp.int32, x.shape, 1)
    return jnp.sum(jnp.where(ids == lane, x, 0.0), axis=1, keepdims=True)


def _swap_halves(x):
    return pltpu.roll(x, LANES // 2, axis=1)


def _head_slope(h):
    return 2.0 ** (-8.0 * (h + 1) / N_HEADS)


def _pack_head_pair(a0, a1, kv_head):
    lo = lax.broadcasted_iota(jnp.int32, a0.shape, 1) < LANES // 2
    if kv_head == 0:
        return jnp.where(lo, a0, _swap_halves(a1))
    return jnp.where(lo, _swap_halves(a0), a1)


N_COUNT_ACC = 4
EXIT_FIRST_STEP = 24
EXIT_CHECK_STEPS = 8

def _alibi_rows(tq):
    pieces, rest = [], jnp.float32(LOG2E)
    for _ in range(N_LOG2E_PIECES):
        piece = rest.astype(MXU_DTYPE).astype(jnp.float32)
        pieces.append(piece)
        rest = rest - piece
    rows = jnp.zeros((N_HEADS, LANES), jnp.float32)
    for h in range(N_HEADS):
        for j, piece in enumerate(pieces):
            rows = rows.at[h, 2 * j].set(_head_slope(h) * POS_RADIX * piece)
            rows = rows.at[h, 2 * j + 1].set(_head_slope(h) * piece)
    return _mxu(jnp.broadcast_to(rows[:, :, None], (N_HEADS, LANES, tq)))


def _attn_prompt_kernel(qt_ref, iqt_ref, iwt_ref, az_ref, kr_ref, ikwr_ref, vtc_ref, aw_ref, o_ref,
                        qw_s, iqw_s, sc_s, mask_s, m_s, l_s, a_s, acc_s, j_s, ot_s, s_s, p_s,
                        *, seq, tq, cks, k_sel):
    i = pl.program_id(1)
    n_kb = (i + 1) * (tq // KEY_BLOCK)
    n_cks = ((i + 1) * tq + cks - 1) // cks
    f32 = jnp.float32

    zeros = jnp.zeros((HEAD_DIM, tq), MXU_DTYPE)
    for h in range(N_HEADS):
        g = h // GROUP
        qw_s[h, g * HEAD_DIM:(g + 1) * HEAD_DIM, :] = qt_ref[0, h * HEAD_DIM:(h + 1) * HEAD_DIM, :]
        qw_s[h, (1 - g) * HEAD_DIM:(2 - g) * HEAD_DIM, :] = zeros
        qw_s[h, D_KV:D_KV + LANES, :] = aw_ref[h]
        iqw_s[h, 0:IDX_DIM, :] = iqt_ref[0, h * IDX_DIM:(h + 1) * IDX_DIM, :]
        iqw_s[h, IDX_DIM:2 * IDX_DIM, :] = zeros

    w_rows = iwt_ref[0] * (N_IDX_HEADS ** -0.5)
    q_pos_c = i * tq + lax.broadcasted_iota(jnp.int32, (cks, tq), 1)
    k_row_c = lax.broadcasted_iota(jnp.int32, (cks, tq), 0)

    def score_chunk(c, carry):
        kk = ikwr_ref[0, pl.ds(pl.multiple_of(c * cks, cks), cks), :]
        acc = jnp.zeros((cks, tq), f32)
        for h in range(N_IDX_HEADS):
            acc = acc + jnp.maximum(_dot(kk, iqw_s[h]), 0.0) * w_rows[h:h + 1, :]
        acc = jnp.where(c * cks + k_row_c <= q_pos_c, acc, -jnp.inf)
        for t in range(cks // KEY_BLOCK):
            sc_s[c * (cks // KEY_BLOCK) + t] = acc[t * KEY_BLOCK:(t + 1) * KEY_BLOCK]
        return carry
    lax.fori_loop(0, n_cks, score_chunk, 0)

    sub_id = lax.broadcasted_iota(jnp.int32, (SUBLANES, tq), 0)

    def count(pred):
        def body(j, parts):
            parts = list(parts)
            for r in range(KEY_BLOCK // SUBLANES):
                rows = sc_s[j, r * SUBLANES:(r + 1) * SUBLANES, :]
                hit = pred(rows, j * KEY_BLOCK + r * SUBLANES + sub_id)
                parts[r % N_COUNT_ACC] = parts[r % N_COUNT_ACC] + jnp.where(hit, 1.0, 0.0)
            return tuple(parts)
        parts = lax.fori_loop(0, n_kb, body, (jnp.zeros((SUBLANES, tq), f32),) * N_COUNT_ACC)
        return jnp.sum(functools.reduce(lambda a, b: a + b, parts), axis=0, keepdims=True)

    def rows8(row):
        return jnp.broadcast_to(row, (SUBLANES, tq))

    def bisect(it, carry):
        tu, cnt_acc = carry
        cand = tu | jnp.left_shift(jnp.int32(1), 31 - it)
        thr8 = rows8(_key_to_threshold(cand))
        cnt = count(lambda s, kp: s >= thr8)
        ok = cnt >= float(k_sel)
        return jnp.where(ok, cand, tu), jnp.where(ok, cnt, cnt_acc)

    n_all = (n_kb * KEY_BLOCK).astype(f32)
    carry = (jnp.zeros((1, tq), jnp.int32), jnp.zeros((1, tq), f32) + n_all)
    carry = lax.fori_loop(0, EXIT_FIRST_STEP, bisect, carry)

    def unresolved(state):
        it, _, cnt = state
        return (it < 32) & (jnp.max(jnp.abs(cnt - float(k_sel))) > 0.0)

    def low_steps(state):
        it, tu, cnt = state
        tu, cnt = lax.fori_loop(it, it + EXIT_CHECK_STEPS, bisect, (tu, cnt))
        return it + EXIT_CHECK_STEPS, tu, cnt

    _, tu, cnt_ge = lax.while_loop(unresolved, low_steps, (jnp.int32(EXIT_FIRST_STEP),) + carry)
    thr = _key_to_threshold(tu)

    j_s[...] = jnp.full((1, tq), seq, jnp.int32)

    @pl.when(jnp.max(cnt_ge) > float(k_sel))
    def _():
        thr8 = rows8(thr)
        need = float(k_sel) - count(lambda s, kp: s > thr8)
        nbits = (seq - 1).bit_length()
        def index_bit(it, p):
            cand = p + jnp.left_shift(jnp.int32(1), nbits - 1 - it)
            cand8 = rows8(cand)
            below = count(lambda s, kp: (s == thr8) & (kp < cand8))
            return jnp.where(below < need, cand, p)
        j_s[...] = lax.fori_loop(0, nbits, index_bit, jnp.zeros((1, tq), jnp.int32))

    m_s[...] = jnp.full(m_s.shape, -jnp.inf, f32)
    l_s[...] = jnp.zeros(l_s.shape, f32)
    acc_s[...] = jnp.zeros(acc_s.shape, f32)
    j_last = j_s[...]
    q_pos = i * tq + lax.broadcasted_iota(jnp.int32, (KEY_BLOCK, tq), 1)
    k_row = lax.broadcasted_iota(jnp.int32, (KEY_BLOCK, tq), 0)

    def attend_block(j, buf):
        k_pos = j * KEY_BLOCK + k_row
        s_idx = sc_s[j]
        sel = ((s_idx > thr) | ((s_idx == thr) & (k_pos <= j_last))) & (k_pos <= q_pos)
        mask_s[buf] = jnp.where(sel, 0.0, MASKED_LOGIT)
        kk = kr_ref[0, pl.ds(pl.multiple_of(j * KEY_BLOCK, KEY_BLOCK), KEY_BLOCK), :]
        vt = vtc_ref[0, j]
        for h in range(N_HEADS):
            s_s[buf, h] = _dot(kk, qw_s[h]) + mask_s[buf]
        for h in range(N_HEADS):
            m_old = m_s[h:h + 1, :]
            m_new = jnp.maximum(m_old, jnp.max(s_s[buf, h], axis=0, keepdims=True))
            alpha = jnp.exp2(m_old - m_new)
            p = jnp.exp2(s_s[buf, h] - m_new)
            l_s[h:h + 1, :] = alpha * l_s[h:h + 1, :] + jnp.sum(p, axis=0, keepdims=True)
            p_s[buf, h] = _mxu(p)
            m_s[h:h + 1, :] = m_new
            a_s[buf, h:h + 1, :] = alpha
        for h in range(N_HEADS):
            g = h // GROUP
            acc_s[h] = (a_s[buf, h:h + 1, :] * acc_s[h]
                        + _dot(vt[g * HEAD_DIM:(g + 1) * HEAD_DIM, :], p_s[buf, h]))

    def attend_pair(t, carry):
        attend_block(2 * t, 0)
        attend_block(2 * t + 1, 1)
        return carry
    lax.fori_loop(0, n_kb // 2, attend_pair, 0)

    @pl.when(n_kb % 2 == 1)
    def _():
        attend_block(n_kb - 1, 0)

    for h in range(N_HEADS):
        ot_s[h * HEAD_DIM:(h + 1) * HEAD_DIM, :] = acc_s[h] / l_s[h:h + 1, :]
    o_ref[0] = _mxu(ot_s[...].T * jax.nn.silu(az_ref[0]))


def _attn_prompt(qt, iqt, iwt, az, k_rows, ikw_rows, vt_blocks, *, tq, cks):
    b, _, seq = qt.shape
    n_kb = seq // KEY_BLOCK
    k_sel = min(TOPK_MAX, seq // 4)
    colblk = lambda height: pl.BlockSpec((1, height, tq), lambda i, j: (i, 0, j))
    whole = lambda shape: pl.BlockSpec((1,) + shape, lambda i, j: (i,) + (0,) * len(shape))
    f32 = jnp.float32
    return pl.pallas_call(
        functools.partial(_attn_prompt_kernel, seq=seq, tq=tq, cks=cks, k_sel=k_sel),
        grid=(b, seq // tq),
        in_specs=[colblk(D_ATT), colblk(D_IQ), colblk(N_IDX_HEADS),
                  pl.BlockSpec((1, tq, D_ATT), lambda i, j: (i, j, 0)),
                  whole((seq, D_KV + LANES)), whole((seq, LANES)), whole((n_kb, D_KV, KEY_BLOCK)),
                  pl.BlockSpec((N_HEADS, LANES, tq), lambda i, j: (0, 0, 0), pipeline_mode=pl.Buffered(1))],
        out_specs=pl.BlockSpec((1, tq, D_ATT), lambda i, j: (i, j, 0)),
        out_shape=jax.ShapeDtypeStruct((b, seq, D_ATT), MXU_DTYPE),
        scratch_shapes=[
            pltpu.VMEM((N_HEADS, D_KV + LANES, tq), MXU_DTYPE),
            pltpu.VMEM((N_IDX_HEADS, LANES, tq), MXU_DTYPE),
            pltpu.VMEM((n_kb, KEY_BLOCK, tq), f32),
            pltpu.VMEM((2, KEY_BLOCK, tq), f32),
            pltpu.VMEM((N_HEADS, tq), f32),
            pltpu.VMEM((N_HEADS, tq), f32),
            pltpu.VMEM((2, N_HEADS, tq), f32),
            pltpu.VMEM((N_HEADS, HEAD_DIM, tq), f32),
            pltpu.VMEM((1, tq), jnp.int32),
            pltpu.VMEM((D_ATT, tq), f32),
            pltpu.VMEM((2, N_HEADS, KEY_BLOCK, tq), f32),
            pltpu.VMEM((2, N_HEADS, KEY_BLOCK, tq), MXU_DTYPE),
        ],
        compiler_params=pltpu.CompilerParams(
            dimension_semantics=("arbitrary", "arbitrary"), vmem_limit_bytes=VMEM_LIMIT),
        name="attn_prompt",
    )(qt, iqt, iwt, az, k_rows, ikw_rows, vt_blocks, _alibi_rows(tq))


def _select_sample_kernel(pt_ref, iq_ref, ikw_ref, ci_hbm, dist_ref, ibuf, sc_s, sem,
                          *, n_pages, t_new, kc, k_sel, group):
    step = pl.program_id(0)
    n_steps = pl.num_programs(0)
    slot = step % 2
    past = n_pages * PAGE_SIZE
    rows = group * t_new
    f32 = jnp.float32

    def page_copy(at_step, at_slot, g, p):
        phys = pt_ref[at_step * group + g, p]
        cols = pl.ds(pl.multiple_of(p * PAGE_SIZE, PAGE_SIZE), PAGE_SIZE)
        return pltpu.make_async_copy(ci_hbm.at[phys], ibuf.at[at_slot, g, :, cols], sem.at[at_slot])

    def start_pages(at_step, at_slot):
        def start_page(p, carry):
            for g in range(group):
                page_copy(at_step, at_slot, g, p).start()
            return carry
        lax.fori_loop(0, n_pages, start_page, 0)

    @pl.when(step == 0)
    def _():
        start_pages(0, 0)

    @pl.when(step + 1 < n_steps)
    def _():
        start_pages(step + 1, 1 - slot)

    def head_sum(r):
        acc = r[0:t_new]
        for h in range(1, N_HEADS):
            acc = acc + r[h * t_new:(h + 1) * t_new]
        return acc

    operands = []
    for g in range(group):
        ikw = ikw_ref[g]
        iq_rows, w_rows = [], []
        for h in range(N_IDX_HEADS):
            it = iq_ref[g, :, (h // 2) * LANES:(h // 2 + 1) * LANES]
            iq_rows.append((it if h % 2 == 0 else _swap_halves(it))[:, 0:IDX_DIM])
            w_rows.append(_lane_column(ikw, IDX_DIM + h) * (N_IDX_HEADS ** -0.5))
        pad = jnp.zeros((LANES - t_new, LANES), f32)
        ik_new = _mxu(jnp.concatenate([ikw, pad], axis=0)[:, 0:IDX_DIM])
        operands.append((_mxu(jnp.concatenate(iq_rows, axis=0)), jnp.concatenate(w_rows, axis=0), ik_new))

    def wait_page(p, carry):
        for g in range(group):
            page_copy(step, slot, g, p).wait()
        return carry
    lax.fori_loop(0, n_pages, wait_page, 0)

    t_id = lax.broadcasted_iota(jnp.int32, (t_new, LANES), 0)
    j_id = lax.broadcasted_iota(jnp.int32, (t_new, LANES), 1)
    for g, (iq_all, w_all, ik_new) in enumerate(operands):
        seq_rows = slice(g * t_new, (g + 1) * t_new)
        for c in range(past // kc):
            cols = slice(c * kc, (c + 1) * kc)
            sc_s[seq_rows, cols] = head_sum(
                jnp.maximum(_dot(iq_all, _mxu(ibuf[slot, g, :, cols])), 0.0) * w_all)
        s_new = head_sum(jnp.maximum(_dot_nt(iq_all, ik_new), 0.0) * w_all)
        sc_s[seq_rows, past:past + LANES] = jnp.where(j_id <= t_id, s_new, -jnp.inf)

    sc = sc_s[...]
    def count(mask):
        return jnp.sum(jnp.where(mask, 1.0, 0.0), axis=1, keepdims=True)

    def bisect(it, carry):
        tu, cnt_acc = carry
        shift = 30 - 2 * it
        for digit in (1, 2, 3):
            cand = tu | jnp.left_shift(jnp.int32(digit), shift)
            cnt = count(sc >= _key_to_threshold(cand))
            ok = cnt >= float(k_sel)
            best = jnp.where(ok, cand, tu if digit == 1 else best)
            cnt_acc = jnp.where(ok, cnt, cnt_acc)
        return best, cnt_acc

    tu, cnt_ge = lax.fori_loop(
        0, 16, bisect,
        (jnp.zeros((rows, 1), jnp.int32), jnp.full((rows, 1), float(past + LANES), f32)))
    thr = _key_to_threshold(tu)

    k_pos = lax.broadcasted_iota(jnp.int32, sc.shape, 1)
    q_pos = past + lax.broadcasted_iota(jnp.int32, sc.shape, 0) % t_new
    nbits = (past + LANES - 1).bit_length()

    def tie_limit():
        need = float(k_sel) - count(sc > thr)
        def index_bit(it, p):
            cand = p + jnp.left_shift(jnp.int32(1), nbits - 1 - it)
            below = count((sc == thr) & (k_pos < cand))
            return jnp.where(below < need, cand, p)
        return lax.fori_loop(0, nbits, index_bit, jnp.zeros((rows, 1), jnp.int32))

    j_last = lax.cond(jnp.max(cnt_ge) > float(k_sel), tie_limit,
                      lambda: jnp.full((rows, 1), past + LANES, jnp.int32))
    sel = ((sc > thr) | ((sc == thr) & (k_pos <= j_last))) & (k_pos <= q_pos)
    dist = jnp.where(sel, (k_pos - q_pos).astype(f32), MASKED_DIST)
    for g in range(group):
        dist_ref[g] = dist[g * t_new:(g + 1) * t_new]


def _select_sample(page_table, iq, ikw, cache_kit, *, kc, group):
    db, t_new, _ = iq.shape
    n_pages = page_table.shape[1]
    past = n_pages * PAGE_SIZE
    k_sel = min(TOPK_MAX, (past + t_new) // 4)
    blk = lambda width: pl.BlockSpec((group, t_new, width), lambda i, pt: (i, 0, 0))
    f32 = jnp.float32
    return pl.pallas_call(
        functools.partial(_select_sample_kernel, n_pages=n_pages, t_new=t_new, kc=kc, k_sel=k_sel,
                          group=group),
        grid_spec=pltpu.PrefetchScalarGridSpec(
            num_scalar_prefetch=1,
            grid=(db // group,),
            in_specs=[blk(D_IQ), blk(LANES), pl.BlockSpec(memory_space=pl.ANY)],
            out_specs=blk(past + LANES),
            scratch_shapes=[
                pltpu.VMEM((2, group, IDX_DIM, past), f32),
                pltpu.VMEM((group * t_new, past + LANES), f32),
                pltpu.SemaphoreType.DMA((2,)),
            ]),
        out_shape=jax.ShapeDtypeStruct((db, t_new, past + LANES), f32),
        compiler_params=pltpu.CompilerParams(
            dimension_semantics=("arbitrary",), vmem_limit_bytes=VMEM_LIMIT),
        name="select_sample",
    )(page_table, iq, ikw, cache_kit)


def _attn_sample_kernel(pt_ref, q_ref, k_ref, v_ref, az_ref, dist_ref, ck_hbm, cv_hbm, o_ref,
                        kbuf, vbuf, s_s, sem,
                        *, n_pages, t_new, kc):
    b = pl.program_id(0)
    nb = pl.num_programs(0)
    slot = b % 2
    past = n_pages * PAGE_SIZE
    n_rows = N_HEADS * t_new
    f32 = jnp.float32

    def page_copies(seq_id, seq_slot, p):
        phys = pt_ref[seq_id, p]
        cols = pl.ds(pl.multiple_of(p * PAGE_SIZE, PAGE_SIZE), PAGE_SIZE)
        return (pltpu.make_async_copy(ck_hbm.at[phys], kbuf.at[seq_slot, :, cols], sem.at[seq_slot, 0]),
                pltpu.make_async_copy(cv_hbm.at[phys], vbuf.at[seq_slot, :, cols], sem.at[seq_slot, 1]))

    def start_pages(seq_id, seq_slot):
        def start_page(p, carry):
            for cp in page_copies(seq_id, seq_slot, p):
                cp.start()
            return carry
        lax.fori_loop(0, n_pages, start_page, 0)

    @pl.when(b == 0)
    def _():
        start_pages(0, 0)

    @pl.when(b + 1 < nb)
    def _():
        start_pages(b + 1, 1 - slot)

    lo = lax.broadcasted_iota(jnp.int32, (t_new, LANES), 1) < LANES // 2
    q_rows = []
    for h in range(N_HEADS):
        qt = q_ref[0, :, (h // 2) * LANES:(h // 2 + 1) * LANES]
        g, par = h // GROUP, h % 2
        if g != par:
            qt = _swap_halves(qt)
        q_rows.append(jnp.where(lo, qt, 0.0) if g == 0 else jnp.where(lo, 0.0, qt))
    q_all = _mxu(jnp.concatenate(q_rows, axis=0))
    slope = jnp.concatenate(
        [jnp.full((t_new, 1), _head_slope(h), f32) for h in range(N_HEADS)], axis=0)

    pad = jnp.zeros((LANES - t_new, LANES), f32)
    k_new = _mxu(jnp.concatenate([k_ref[0], pad], axis=0))
    v_new = _mxu(jnp.concatenate([v_ref[0], pad], axis=0))

    def wait_page(p, carry):
        for cp in page_copies(b, slot, p):
            cp.wait()
        return carry
    lax.fori_loop(0, n_pages, wait_page, 0)

    dist = dist_ref[0]

    def logits(s, d):
        return s + slope * jnp.concatenate([d] * N_HEADS, axis=0)

    m = jnp.full((n_rows, 1), -jnp.inf, f32)
    for c in range(past // kc):
        cols = slice(c * kc, (c + 1) * kc)
        s = logits(_dot(q_all, _mxu(kbuf[slot, :, cols])), dist[:, cols])
        s_s[:, cols] = s
        m = jnp.maximum(m, jnp.max(s, axis=1, keepdims=True))
    s = logits(_dot_nt(q_all, k_new), dist[:, past:past + LANES])
    s_s[:, past:past + LANES] = s
    m = jnp.maximum(m, jnp.max(s, axis=1, keepdims=True))

    l = jnp.zeros((n_rows, 1), f32)
    o = jnp.zeros((n_rows, LANES), f32)
    for c in range(past // kc):
        cols = slice(c * kc, (c + 1) * kc)
        p = jnp.exp(s_s[:, cols] - m)
        l = l + jnp.sum(p, axis=1, keepdims=True)
        o = o + _dot_nt(_mxu(p), _mxu(vbuf[slot, :, cols]))
    p = jnp.exp(s_s[:, past:past + LANES] - m)
    l = l + jnp.sum(p, axis=1, keepdims=True)
    o = (o + _dot(_mxu(p), v_new)) / l

    for j in range(N_HEADS // 2):
        tile = _pack_head_pair(
            o[(2 * j) * t_new:(2 * j + 1) * t_new], o[(2 * j + 1) * t_new:(2 * j + 2) * t_new],
            (2 * j) // GROUP)
        gate = jax.nn.silu(az_ref[0, :, j * LANES:(j + 1) * LANES])
        o_ref[0, :, j * LANES:(j + 1) * LANES] = _mxu(tile * gate)


def _attn_sample(page_table, q, k, v, az, dist, cache_kt, cache_vt, *, kc):
    db, t_new, _ = q.shape
    n_pages = page_table.shape[1]
    past = n_pages * PAGE_SIZE
    blk = lambda width: pl.BlockSpec((1, t_new, width), lambda i, pt: (i, 0, 0))
    hbm = pl.BlockSpec(memory_space=pl.ANY)
    f32 = jnp.float32
    return pl.pallas_call(
        functools.partial(_attn_sample_kernel, n_pages=n_pages, t_new=t_new, kc=kc),
        grid_spec=pltpu.PrefetchScalarGridSpec(
            num_scalar_prefetch=1,
            grid=(db,),
            in_specs=[blk(D_ATT), blk(D_KV), blk(D_KV), blk(D_ATT), blk(past + LANES), hbm, hbm],
            out_specs=blk(D_ATT),
            scratch_shapes=[
                pltpu.VMEM((2, D_KV, past), f32),
                pltpu.VMEM((2, D_KV, past), f32),
                pltpu.VMEM((N_HEADS * t_new, past + LANES), f32),
                pltpu.SemaphoreType.DMA((2, 2)),
            ]),
        out_shape=jax.ShapeDtypeStruct((db, t_new, D_ATT), MXU_DTYPE),
        compiler_params=pltpu.CompilerParams(
            dimension_semantics=("arbitrary",), vmem_limit_bytes=VMEM_LIMIT),
        name="attn_sample",
    )(page_table, q, k, v, az, dist, cache_kt, cache_vt)


def _prepare_weights(norm_gain, w_in, conv_w, w_up_conv, w_up_attn, w_out, final_gain):
    split = C_IKW + IDX_DIM + N_IDX_HEADS
    pad = jnp.zeros((D_MODEL, D_IN_PAD - D_IN_RAW), w_in.dtype)
    w_pad = _mxu(jnp.concatenate([w_in[:, :split], pad, w_in[:, split:]], axis=1))
    used_t = R_IKW + IDX_DIM + N_IDX_HEADS
    w_in_t = w_in.T
    w_t = _mxu(jnp.concatenate(
        [w_in_t[C_Q:C_Q + D_ATT], w_in_t[C_IQ:C_IQ + D_IQ], w_in_t[C_K:C_K + 2 * D_KV],
         w_in_t[C_IKW:split], jnp.zeros((D_T - used_t, D_MODEL), w_in.dtype)], axis=0))
    return (norm_gain, w_pad, w_t, conv_w), (_mxu(w_up_conv), _mxu(w_up_attn), _mxu(w_out), final_gain)


def _prompt_layer(x, proj_w, merge_w, *, tm_proj, tm_merge, tq, cks):
    b, t, _ = x.shape
    prev = jnp.zeros((b, CONV_WIDTH - 1, D_CONV), x.dtype)
    yc, az, gg, cs, qt, iqt, kt, vt, kit, iwt, k_rows, ikw_rows, vt_blocks = _project(
        x, prev, *proj_w, tm=tm_proj, tokens_on_lanes=True)
    ya = _attn_prompt(qt, iqt, iwt, az, k_rows, ikw_rows, vt_blocks, tq=tq, cks=cks)
    flat = lambda a: a.reshape(b * t, a.shape[-1])
    y = _merge(flat(x), flat(yc), flat(ya), flat(gg), *merge_w, tm=tm_merge)
    heads = lambda a: a.reshape(b, N_KV_HEADS, HEAD_DIM, t).transpose(0, 3, 1, 2)[None]
    return y.reshape(b, t, D_MODEL), heads(kt), heads(vt), kit.transpose(0, 2, 1)[None], cs[None]


def _sample_layer(x, prev, page_table, cache_k, cache_v, cache_kidx, proj_w, merge_w,
                  *, tm_proj, tm_merge, kc, group):
    b, t, _ = x.shape
    n_phys = cache_k.shape[0]
    zeros = jnp.zeros((b, t, D_CONV), prev.dtype)
    one_back = zeros.at[:, 0].set(prev[:, 1])
    two_back = zeros.at[:, 0].set(prev[:, 0]).at[:, 1].set(prev[:, 1])
    tiles = (b * t) // tm_proj
    tiled = lambda a: a.reshape(tiles, tm_proj, a.shape[-1])
    outs = _project(tiled(x), jnp.stack([tiled(one_back), tiled(two_back)], axis=1), *proj_w,
                    tm=tm_proj, tokens_on_lanes=False, rows_per_seq=t)
    yc, az, gg, u, q, iq, ikw, k, v = [a.reshape(b, t, a.shape[-1]) for a in outs]
    cs = u[:, t - (CONV_WIDTH - 1):]
    cache_kt = cache_k.transpose(0, 2, 3, 1).reshape(n_phys, D_KV, PAGE_SIZE)
    cache_vt = cache_v.transpose(0, 2, 3, 1).reshape(n_phys, D_KV, PAGE_SIZE)
    cache_kit = cache_kidx.transpose(0, 2, 1)
    dist = _select_sample(page_table, iq, ikw, cache_kit, kc=kc, group=group)
    ya = _attn_sample(page_table, q, k, v, az, dist, cache_kt, cache_vt, kc=kc)
    flat = lambda a: a.reshape(b * t, a.shape[-1])
    y = _merge(flat(x), flat(yc), flat(ya), flat(gg), *merge_w, tm=tm_merge)
    heads = lambda a: a.reshape(1, b, t, N_KV_HEADS, HEAD_DIM)
    return y.reshape(b, t, D_MODEL), heads(k), heads(v), ikw[None, :, :, :IDX_DIM], cs[None]


def _tiles(prompt_shape, sample_shape, n_pages):
    b, seq, _ = prompt_shape
    db, t_new, _ = sample_shape
    row_tile = 2 * KEY_BLOCK
    tiles = dict(
        prompt=dict(tm_proj=min(row_tile, seq), tm_merge=min(row_tile, b * seq), tq=KEY_BLOCK,
                    cks=min(row_tile, seq)),
        sample=dict(tm_proj=min(row_tile, db * t_new), tm_merge=min(row_tile, db * t_new),
                    kc=min(8 * LANES, n_pages * PAGE_SIZE), group=min(8, db)))
    assert seq % tiles["prompt"]["tm_proj"] == 0 and seq % KEY_BLOCK == 0 and seq % tiles["prompt"]["cks"] == 0
    assert (b * seq) % tiles["prompt"]["tm_merge"] == 0
    assert (db * t_new) % tiles["sample"]["tm_proj"] == 0 and tiles["sample"]["tm_proj"] % t_new == 0
    assert (n_pages * PAGE_SIZE) % tiles["sample"]["kc"] == 0 and db % tiles["sample"]["group"] == 0
    assert min(TOPK_MAX, seq // 4) <= KEY_BLOCK * (seq // KEY_BLOCK)
    return tiles


def kernel(x_prompt, x_sample, cache_k, cache_v, cache_kidx, state_conv, page_table, norm_gain, w_in,
           conv_w, w_up_conv, w_up_attn, w_out, final_gain):
    assert norm_gain.shape[0] == 1, "single-layer step"
    proj_w, merge_w = _prepare_weights(norm_gain[0], w_in[0], conv_w[0], w_up_conv[0], w_up_attn[0],
                                       w_out[0], final_gain)
    tiles = _tiles(x_prompt.shape, x_sample.shape, page_table.shape[1])
    yp, kp, vp, ip, cp = _prompt_layer(x_prompt, proj_w, merge_w, **tiles["prompt"])
    ys, ks, vs, is_, cs = _sample_layer(x_sample, state_conv[0], page_table, cache_k[0], cache_v[0],
                                        cache_kidx[0], proj_w, merge_w, **tiles["sample"])
    return (yp, ys, kp, vp, ip, cp, ks, vs, is_, cs)
```
